```python
import math
import jax, jax.numpy as jnp
from jax import lax
import numpy as np

D_MODEL = 2048
BATCH = 2
SEQ = 4096
DEPTH = 2
DEC_BATCH = 4
DEC_SEQ = 2048
PAST_LEN = 128

D_MIX = D_MODEL
D_GROUP = D_MIX // 4
POOL_WINDOWS = (2, 4, 8, 16)
N_POOL = len(POOL_WINDOWS)
POOL_CH = D_GROUP // N_POOL
HEAD_DIM = 64
N_Q_HEADS = D_GROUP // HEAD_DIM
N_KV_HEADS = 2
GQA_GROUP = N_Q_HEADS // N_KV_HEADS
WINDOW = 128
BLOCK = 128
ROPE_DIM = HEAD_DIM // 4
ROPE_THETA = 500000.0
CONV_W = 3
HY_ORDER = 2
HY_EMB = 33
HY_BANDS = (HY_EMB - 1) // 2
HY_HIDDEN = 64
HY_FILTERS = HY_ORDER * 2 * D_GROUP
HY_FAST_DECAY_PCT = 0.3
HY_SLOW_DECAY_PCT = 1.5
HY_DECAY_TARGET = 1e-2
D_FF = ((8 * D_MODEL // 3 + 255) // 256) * 256
NORM_EPS = 1e-6

OFF_A = 0
OFF_Q = OFF_A + D_GROUP
OFF_K = OFF_Q + N_Q_HEADS * HEAD_DIM
OFF_V = OFF_K + N_KV_HEADS * HEAD_DIM
OFF_C = OFF_V + N_KV_HEADS * HEAD_DIM
OFF_D = OFF_C + 3 * D_GROUP
D_IN = OFF_D + 3 * D_GROUP

kernel_name = "hymba_parallel_pool_swa_conv_hyena_encoder"


def rms_norm(x, g):
    xf = x.astype(jnp.float32)
    y = xf * lax.rsqrt(jnp.mean(xf * xf, axis=-1, keepdims=True) + NORM_EPS)
    return (y * g.astype(jnp.float32)).astype(x.dtype)


def short_conv(u, w):
    L = u.shape[1]
    up = jnp.pad(u, ((0, 0), (1, 1), (0, 0)))
    return up[:, 0:L] * w[0] + up[:, 1:L + 1] * w[1] + up[:, 2:L + 2] * w[2]


def pool_mixer(p, pool_w, pool_scale):
    B, L, _ = p.shape
    pg = p.astype(jnp.float32).reshape(B, L, N_POOL, POOL_CH)
    cs = jnp.concatenate([jnp.zeros((B, 1, N_POOL, POOL_CH), jnp.float32), jnp.cumsum(pg, axis=1)], axis=1)
    t = jnp.arange(L)[:, None]
    half = jnp.array(POOL_WINDOWS, dtype=jnp.int32)[None, :] // 2
    lo = jnp.clip(t - half, 0, L)
    hi = jnp.clip(t + half, 0, L)
    gidx = jnp.arange(N_POOL)[None, :]
    pooled = (cs[:, hi, gidx] - cs[:, lo, gidx]) / (hi - lo).astype(jnp.float32)[None, :, :, None]
    d = pooled - pg
    out = jnp.einsum('blgc,gcd->blgd', d, pool_w.astype(jnp.float32))
    return (out.reshape(B, L, D_GROUP) * pool_scale.astype(jnp.float32)).astype(p.dtype)


def rotate_half(x):
    x1, x2 = jnp.split(x, 2, axis=-1)
    return jnp.concatenate([-x2, x1], axis=-1)


def partial_rope(x):
    L = x.shape[1]
    inv_freq = ROPE_THETA ** (-jnp.arange(0, ROPE_DIM, 2, dtype=jnp.float32) / ROPE_DIM)
    ang = jnp.arange(L, dtype=jnp.float32)[:, None] * inv_freq[None, :]
    ang = jnp.concatenate([ang, ang], axis=-1)[None, :, None, :]
    cos = jnp.cos(ang).astype(x.dtype)
    sin = jnp.sin(ang).astype(x.dtype)
    rot = x[..., :ROPE_DIM]
    return jnp.concatenate([rot * cos + rotate_half(rot) * sin, x[..., ROPE_DIM:]], axis=-1)


def window_attention(q, k, v, sink):
    B, L = q.shape[0], q.shape[1]
    nb = L // BLOCK
    qb = q.reshape(B, nb, BLOCK, N_KV_HEADS, GQA_GROUP, HEAD_DIM)

    def band(t):
        tp = jnp.pad(t, ((0, 0), (BLOCK, BLOCK), (0, 0), (0, 0))).reshape(B, nb + 2, BLOCK, N_KV_HEADS, HEAD_DIM)
        return jnp.concatenate([tp[:, :-2], tp[:, 1:-1], tp[:, 2:]], axis=2)

    kb, vb = band(k), band(v)
    s = jnp.einsum('bnrhgd,bnchd->bnhgrc', qb, kb, preferred_element_type=jnp.float32) * (HEAD_DIM ** -0.5)
    r = jnp.arange(BLOCK)
    c = jnp.arange(3 * BLOCK)
    rel = c[None, :] - BLOCK - r[:, None]
    keypos = jnp.arange(nb)[:, None] * BLOCK - BLOCK + c[None, :]
    mask = (jnp.abs(rel) <= WINDOW)[None, :, :] & ((keypos >= 0) & (keypos < L))[:, None, :]
    s = jnp.where(mask[None, :, None, None, :, :], s, -jnp.inf)
    sink_l = sink.astype(jnp.float32).reshape(N_KV_HEADS, GQA_GROUP)[None, None, :, :, None, None]
    m = jnp.maximum(jnp.max(s, axis=-1, keepdims=True), sink_l)
    e = jnp.exp(s - m)
    p = e / (jnp.sum(e, axis=-1, keepdims=True) + jnp.exp(sink_l - m))
    o = jnp.einsum('bnhgrc,bnchd->bnrhgd', p.astype(v.dtype), vb)
    return o.reshape(B, L, N_Q_HEADS * HEAD_DIM)


def gated_short_conv(h, bg, cg, w):
    return bg * short_conv(cg * h, w)


def hyena_filters(L, w1, b1, w2, b2, w3, freq, decay):
    f32 = jnp.float32
    t = jnp.linspace(0.0, 1.0, L, dtype=f32)[:, None]
    bands = jnp.linspace(1e-4, HY_BANDS - 1, HY_BANDS, dtype=f32)[None, :]
    wpos = (2.0 * math.pi / L) * jnp.arange(L, dtype=f32)[:, None]
    z = jnp.concatenate([t, jnp.cos(bands * wpos), -jnp.sin(bands * wpos)], axis=-1)
    fr = freq.astype(f32)
    h = jnp.sin(fr * (z @ w1.astype(f32) + b1.astype(f32)))
    h = jnp.sin(fr * (h @ w2.astype(f32) + b2.astype(f32)))
    h = (h @ w3.astype(f32)) * jnp.exp(-t * jnp.abs(decay.astype(f32)))
    h = h.reshape(L, HY_ORDER, 2, D_GROUP)
    fwd, bwd = h[:, :, 0], h[:, :, 1]
    k = jnp.concatenate([fwd, jnp.zeros((1, HY_ORDER, D_GROUP), f32), bwd[1:][::-1]], axis=0)
    return k / jnp.sum(jnp.abs(k), axis=0, keepdims=True)


def long_conv(u, kf):
    L = u.shape[1]
    uf = jnp.fft.rfft(u, n=2 * L, axis=1)
    return jnp.fft.irfft(uf * kf[None], n=2 * L, axis=1)[:, :L]


def hyena_mixer(zd, short_w, w1, b1, w2, b2, w3, freq, decay, bias):
    L = zd.shape[1]
    z = short_conv(zd, short_w).astype(jnp.float32)
    v, g1, g2 = jnp.split(z, 3, axis=-1)
    kf = jnp.fft.rfft(hyena_filters(L, w1, b1, w2, b2, w3, freq, decay), axis=0)
    bias = bias.astype(jnp.float32)
    u = v
    for i, g in enumerate((g1, g2)):
        u = g * (long_conv(u, kf[:, i]) + bias[i] * u)
    return u.astype(zd.dtype)


def trunk(x, mix_norm_g, w_in, pool_w, pool_scale, attn_sink, sconv_w, hy_short_w,
          hy_w1, hy_b1, hy_w2, hy_b2, hy_w3, hy_freq, hy_decay, hy_bias,
          w_out, ffn_norm_g, w_gate_up, w_down, final_norm_g):
    B, L, _ = x.shape
    for l in range(DEPTH):
        hn = rms_norm(x, mix_norm_g[l])
        z = hn @ w_in[l]
        za = z[..., OFF_A:OFF_Q]
        q = partial_rope(z[..., OFF_Q:OFF_K].reshape(B, L, N_Q_HEADS, HEAD_DIM))
        k = partial_rope(z[..., OFF_K:OFF_V].reshape(B, L, N_KV_HEADS, HEAD_DIM))
        v = z[..., OFF_V:OFF_C].reshape(B, L, N_KV_HEADS, HEAD_DIM)
        ch, cb, cc = jnp.split(z[..., OFF_C:OFF_D], 3, axis=-1)
        zd = z[..., OFF_D:D_IN]
        out_a = pool_mixer(za, pool_w[l], pool_scale[l])
        out_b = window_attention(q, k, v, attn_sink[l])
        out_c = gated_short_conv(ch, cb, cc, sconv_w[l])
        out_d = hyena_mixer(zd, hy_short_w[l], hy_w1[l], hy_b1[l], hy_w2[l], hy_b2[l], hy_w3[l],
                            hy_freq[l], hy_decay[l], hy_bias[l])
        mix = jnp.concatenate([out_a.astype(x.dtype), out_b.astype(x.dtype),
                               out_c.astype(x.dtype), out_d.astype(x.dtype)], axis=-1)
        x = x + mix @ w_out[l]
        h2 = rms_norm(x, ffn_norm_g[l])
        gate, up = jnp.split(h2 @ w_gate_up[l], 2, axis=-1)
        x = x + (jax.nn.silu(gate) * up) @ w_down[l]
    return rms_norm(x, final_norm_g)


def setup_inputs(seed: int = 0) -> dict:
    key = jax.random.key(seed)
    ks = jax.random.split(key, 24)
    f32 = jnp.float32
    nrm = lambda k, shape, s: jax.random.normal(k, shape, f32) * s
    max_decay = math.log(HY_DECAY_TARGET) / HY_FAST_DECAY_PCT
    min_decay = math.log(HY_DECAY_TARGET) / HY_SLOW_DECAY_PCT
    decay0 = jnp.tile(jnp.abs(jnp.linspace(min_decay, max_decay, D_GROUP, dtype=f32)), HY_ORDER * 2)
    return {
        "x_prompt": nrm(ks[0], (BATCH, SEQ, D_MODEL), 1.0),
        "x_sample": nrm(ks[1], (DEC_BATCH, DEC_SEQ, D_MODEL), 1.0),
        "mix_norm_g": 1.0 + nrm(ks[2], (DEPTH, D_MODEL), 0.02),
        "w_in": nrm(ks[3], (DEPTH, D_MODEL, D_IN), D_MODEL ** -0.5),
        "pool_w": nrm(ks[4], (DEPTH, N_POOL, POOL_CH, POOL_CH), POOL_CH ** -0.5),
        "pool_scale": 1.0 + nrm(ks[5], (DEPTH, D_GROUP), 0.02),
        "attn_sink": nrm(ks[6], (DEPTH, N_Q_HEADS), 0.5),
        "sconv_w": nrm(ks[7], (DEPTH, CONV_W, D_GROUP), CONV_W ** -0.5),
        "hy_short_w": nrm(ks[8], (DEPTH, CONV_W, 3 * D_GROUP), CONV_W ** -0.5),
        "hy_w1": nrm(ks[9], (DEPTH, HY_EMB, HY_HIDDEN), HY_EMB ** -0.5),
        "hy_b1": nrm(ks[10], (DEPTH, HY_HIDDEN), 0.1),
        "hy_w2": nrm(ks[11], (DEPTH, HY_HIDDEN, HY_HIDDEN), HY_HIDDEN ** -0.5),
        "hy_b2": nrm(ks[12], (DEPTH, HY_HIDDEN), 0.1),
        "hy_w3": nrm(ks[13], (DEPTH, HY_HIDDEN, HY_FILTERS), HY_HIDDEN ** -0.5),
        "hy_freq": 1.0 + nrm(ks[14], (DEPTH, HY_HIDDEN), 0.1),
        "hy_decay": decay0[None, :] * (1.0 + nrm(ks[15], (DEPTH, HY_FILTERS), 0.05)),
        "hy_bias": nrm(ks[16], (DEPTH, HY_ORDER, D_GROUP), 0.5),
        "w_out": nrm(ks[17], (DEPTH, D_MIX, D_MODEL), D_MIX ** -0.5),
        "ffn_norm_g": 1.0 + nrm(ks[18], (DEPTH, D_MODEL), 0.02),
        "w_gate_up": nrm(ks[19], (DEPTH, D_MODEL, 2 * D_FF), D_MODEL ** -0.5),
        "w_down": nrm(ks[20], (DEPTH, D_FF, D_MODEL), D_FF ** -0.5),
        "final_norm_g": 1.0 + nrm(ks[21], (D_MODEL,), 0.02),
    }


def reference(x_prompt, x_sample, mix_norm_g, w_in, pool_w, pool_scale, attn_sink, sconv_w, hy_short_w,
              hy_w1, hy_b1, hy_w2, hy_b2, hy_w3, hy_freq, hy_decay, hy_bias,
              w_out, ffn_norm_g, w_gate_up, w_down, final_norm_g):
    y_prompt = trunk(x_prompt, mix_norm_g, w_in, pool_w, pool_scale, attn_sink, sconv_w, hy_short_w,
                     hy_w1, hy_b1, hy_w2, hy_b2, hy_w3, hy_freq, hy_decay, hy_bias,
                     w_out, ffn_norm_g, w_gate_up, w_down, final_norm_g)
    y_sample = trunk(x_sample, mix_norm_g, w_in, pool_w, pool_scale, attn_sink, sconv_w, hy_short_w,
                     hy_w1, hy_b1, hy_w2, hy_b2, hy_w3, hy_freq, hy_decay, hy_bias,
                     w_out, ffn_norm_g, w_gate_up, w_down, final_norm_g)
    return (y_prompt, y_sample)
```

```python
import functools
import math

import numpy as np
import jax
import jax.numpy as jnp
from jax import lax
from jax.experimental import pallas as pl
from jax.experimental.pallas import tpu as pltpu

F32 = jnp.float32
BF16 = jnp.bfloat16

D_MODEL = 2048
DEPTH = 2
D_GROUP = 512
POOL_WINDOWS = (2, 4, 8, 16)
POOL_CH = 128
HEAD_DIM = 64
N_Q_HEADS = 8
N_KV_HEADS = 2
GQA_GROUP = 4
BLOCK = 128
ROPE_DIM = 16
ROPE_THETA = 500000.0
HY_ORDER = 2
HY_EMB = 33
HY_BANDS = 16
HY_HIDDEN = 64
D_FF = 5632
NORM_EPS = 1e-6
OFF_Q = 512
OFF_K = 1024
OFF_V = 1152
OFF_C = 1280
OFF_D = 2816
D_IN = 4352

LANES = 128
SUBLANES = 8
DFT_N = 256
VMEM_LIMIT = 56 * 2 ** 20


def _params(sem, vmem=VMEM_LIMIT):
    return pltpu.CompilerParams(dimension_semantics=sem, vmem_limit_bytes=vmem)


def _rms(x, g):
    ms = jnp.mean(x * x, axis=-1, keepdims=True)
    return x * lax.rsqrt(ms + NORM_EPS) * g


def _inproj_body(x_ref, g_ref, w_ref, o_ref, hn_ref):
    @pl.when(pl.program_id(1) == 0)
    def _():
        hn_ref[...] = _rms(x_ref[...], g_ref[...]).astype(BF16)

    o_ref[...] = jnp.dot(hn_ref[...], w_ref[...], preferred_element_type=F32)


def _inproj(x2d, g, w):
    T = x2d.shape[0]
    tm, tn = 512, D_IN // 2
    return pl.pallas_call(
        _inproj_body,
        grid=(T // tm, D_IN // tn),
        in_specs=[
            pl.BlockSpec((tm, D_MODEL), lambda i, j: (i, 0)),
            pl.BlockSpec((1, D_MODEL), lambda i, j: (0, 0)),
            pl.BlockSpec((D_MODEL, tn), lambda i, j: (0, j)),
        ],
        out_specs=pl.BlockSpec((tm, tn), lambda i, j: (i, j)),
        out_shape=jax.ShapeDtypeStruct((T, D_IN), F32),
        scratch_shapes=[pltpu.VMEM((tm, D_MODEL), BF16)],
        compiler_params=_params(("parallel", "arbitrary")),
        name="inproj",
    )(x2d, g, w)


def _outproj_body(x_ref, a_ref, b_ref, c_ref, d_ref, w_ref, o_ref):
    acc = x_ref[...]
    for k, m_ref in enumerate((a_ref, b_ref, c_ref, d_ref)):
        acc = acc + jnp.dot(m_ref[...], w_ref[k * D_GROUP:(k + 1) * D_GROUP, :], preferred_element_type=F32)
    o_ref[...] = acc


def _outproj(x2d, mixes, w):
    T = x2d.shape[0]
    tm = 512
    mix_spec = pl.BlockSpec((tm, D_GROUP), lambda i: (i, 0))
    return pl.pallas_call(
        _outproj_body,
        grid=(T // tm,),
        in_specs=[pl.BlockSpec((tm, D_MODEL), lambda i: (i, 0)), mix_spec, mix_spec, mix_spec, mix_spec,
                  pl.BlockSpec((D_MODEL, D_MODEL), lambda i: (0, 0))],
        out_specs=pl.BlockSpec((tm, D_MODEL), lambda i: (i, 0)),
        out_shape=jax.ShapeDtypeStruct((T, D_MODEL), F32),
        compiler_params=_params(("parallel",)),
        name="outproj",
    )(x2d, *mixes, w)


def _ffn_body(x_ref, g_ref, wg_ref, wu_ref, wd_ref, gf_ref, o_ref, h_ref, *, final_norm):
    j = pl.program_id(1)

    @pl.when(j == 0)
    def _():
        x = x_ref[...]
        h_ref[...] = _rms(x, g_ref[...]).astype(BF16)
        o_ref[...] = x

    h = h_ref[...]
    gate = jnp.dot(h, wg_ref[...], preferred_element_type=F32)
    up = jnp.dot(h, wu_ref[...], preferred_element_type=F32)
    act = (gate * jax.nn.sigmoid(gate) * up).astype(BF16)
    o_ref[...] += jnp.dot(act, wd_ref[...], preferred_element_type=F32)

    if final_norm:
        @pl.when(j == pl.num_programs(1) - 1)
        def _():
            o_ref[...] = _rms(o_ref[...], gf_ref[...])


def _ffn(x2d, g, w_gate_up, w_down, gf, final_norm):
    T = x2d.shape[0]
    tm, tf = 512, 512
    nf = D_FF // tf
    return pl.pallas_call(
        functools.partial(_ffn_body, final_norm=final_norm),
        grid=(T // tm, nf),
        in_specs=[
            pl.BlockSpec((tm, D_MODEL), lambda i, j: (i, 0)),
            pl.BlockSpec((1, D_MODEL), lambda i, j: (0, 0)),
            pl.BlockSpec((D_MODEL, tf), lambda i, j: (0, j)),
            pl.BlockSpec((D_MODEL, tf), lambda i, j: (0, j + nf)),
            pl.BlockSpec((tf, D_MODEL), lambda i, j: (j, 0)),
            pl.BlockSpec((1, D_MODEL), lambda i, j: (0, 0)),
        ],
        out_specs=pl.BlockSpec((tm, D_MODEL), lambda i, j: (i, 0)),
        out_shape=jax.ShapeDtypeStruct((T, D_MODEL), F32),
        scratch_shapes=[pltpu.VMEM((tm, D_MODEL), BF16)],
        compiler_params=_params(("parallel", "arbitrary")),
        name="ffn",
    )(x2d, g, w_gate_up, w_gate_up, w_down, gf)


def _shift_rows(x, s):
    n = x.shape[0]
    r = pltpu.roll(x, s % n, axis=0)
    t = lax.broadcasted_iota(jnp.int32, x.shape, 0)
    keep = (t >= s) if s > 0 else (t < n + s)
    return jnp.where(keep, r, 0.0)


def _conv3(x, w_ref):
    return _shift_rows(x, 1) * w_ref[0:1, :] + x * w_ref[1:2, :] + _shift_rows(x, -1) * w_ref[2:3, :]


POOL_PAD = 16


def _pool_body(p_ref, w_ref, s_ref, o_ref, pad_ref, *, seq):
    c = pl.program_id(1)
    p = p_ref[...]
    zeros = jnp.zeros((POOL_PAD, LANES), F32)
    pad_ref[0:POOL_PAD, :] = zeros
    pad_ref[POOL_PAD + seq:2 * POOL_PAD + seq, :] = zeros
    pad_ref[POOL_PAD:POOL_PAD + seq, :] = p
    n = seq + 2 * POOL_PAD
    t = lax.broadcasted_iota(jnp.int32, (seq, 1), 0)

    for k, win in enumerate(POOL_WINDOWS):
        @pl.when(c == k)
        def _(win=win):
            s = pad_ref[...]
            s = s + pltpu.roll(s, 1, axis=0)
            step = 1
            while 2 * step < win:
                s = pltpu.roll(s, step, axis=0) + pltpu.roll(s, n - step, axis=0)
                step *= 2
            s = s[POOL_PAD:POOL_PAD + seq, :]
            half = win // 2
            cnt = (jnp.minimum(t + half, seq) - jnp.maximum(t - half, 0)).astype(F32)
            d = s / cnt - p
            out = jnp.dot(d.astype(BF16), w_ref[...], preferred_element_type=F32)
            o_ref[...] = (out * s_ref[...]).astype(o_ref.dtype)


def _pool(z2d, pool_w, pool_scale, nb, seq):
    return pl.pallas_call(
        functools.partial(_pool_body, seq=seq),
        grid=(nb, len(POOL_WINDOWS)),
        in_specs=[
            pl.BlockSpec((seq, POOL_CH), lambda b, c: (b, c)),
            pl.BlockSpec((None, POOL_CH, POOL_CH), lambda b, c: (c, 0, 0)),
            pl.BlockSpec((1, POOL_CH), lambda b, c: (0, c)),
        ],
        out_specs=pl.BlockSpec((seq, POOL_CH), lambda b, c: (b, c)),
        out_shape=jax.ShapeDtypeStruct((nb * seq, D_GROUP), BF16),
        scratch_shapes=[pltpu.VMEM((seq + 2 * POOL_PAD, LANES), F32)],
        compiler_params=_params(("parallel", "parallel")),
        name="pool_mixer",
    )(z2d, pool_w, pool_scale)


def _sconv_body(h_ref, b_ref, c_ref, w_ref, o_ref):
    o_ref[...] = (b_ref[...] * _conv3(c_ref[...] * h_ref[...], w_ref)).astype(o_ref.dtype)


def _sconv(z2d, w, nb, seq):
    nc = D_GROUP // LANES
    c0 = OFF_C // LANES
    return pl.pallas_call(
        _sconv_body,
        grid=(nb, nc),
        in_specs=[
            pl.BlockSpec((seq, LANES), lambda b, c: (b, c0 + c)),
            pl.BlockSpec((seq, LANES), lambda b, c: (b, c0 + nc + c)),
            pl.BlockSpec((seq, LANES), lambda b, c: (b, c0 + 2 * nc + c)),
            pl.BlockSpec((3, LANES), lambda b, c: (0, c)),
        ],
        out_specs=pl.BlockSpec((seq, LANES), lambda b, c: (b, c)),
        out_shape=jax.ShapeDtypeStruct((nb * seq, D_GROUP), BF16),
        compiler_params=_params(("parallel", "parallel")),
        name="gated_sconv",
    )(z2d, z2d, z2d, w)


def _hshort_body(z_ref, w_ref, o_ref):
    o_ref[...] = _conv3(z_ref[...], w_ref)


def _hshort(z2d, w, nb, seq):
    nc = 3 * D_GROUP // LANES
    c0 = OFF_D // LANES
    return pl.pallas_call(
        _hshort_body,
        grid=(nb, nc),
        in_specs=[pl.BlockSpec((seq, LANES), lambda b, c: (b, c0 + c)),
                  pl.BlockSpec((3, LANES), lambda b, c: (0, c))],
        out_specs=pl.BlockSpec((seq, LANES), lambda b, c: (b, c)),
        out_shape=jax.ShapeDtypeStruct((nb * seq, 3 * D_GROUP), F32),
        compiler_params=_params(("parallel", "parallel")),
        name="hyena_short_conv",
    )(z2d, w)


def _rope_tables(seq):
    inv_freq = ROPE_THETA ** (-np.arange(0, ROPE_DIM, 2, dtype=np.float64) / ROPE_DIM)
    ang = np.arange(seq, dtype=np.float64)[:, None] * inv_freq[None, :]
    half = ROPE_DIM // 2
    cos = np.ones((seq, HEAD_DIM))
    s_lo = np.zeros((seq, HEAD_DIM))
    s_hi = np.zeros((seq, HEAD_DIM))
    cos[:, :half] = np.cos(ang)
    cos[:, half:ROPE_DIM] = np.cos(ang)
    s_lo[:, half:ROPE_DIM] = np.sin(ang)
    s_hi[:, :half] = -np.sin(ang)
    rep = LANES // HEAD_DIM
    return tuple(jnp.asarray(np.tile(a, (1, rep)), F32) for a in (cos, s_lo, s_hi))


def _rope(x, cos, s_lo, s_hi):
    half = ROPE_DIM // 2
    return x * cos + pltpu.roll(x, half, axis=1) * s_lo + pltpu.roll(x, LANES - half, axis=1) * s_hi


def _attn_body(q_ref, kp_ref, kc_ref, kn_ref, vp_ref, vc_ref, vn_ref, cq_ref, cp_ref, cn_ref, sink_ref, o_ref):
    n = pl.program_id(1)
    nblk = pl.num_programs(1)
    tq = tuple(r[...] for r in cq_ref)
    rows = GQA_GROUP * BLOCK

    q = q_ref[...]
    qg = [_rope(q[:, g * LANES:(g + 1) * LANES], *tq) * (HEAD_DIM ** -0.5) for g in range(D_GROUP // LANES)]
    qh = [qg[h // 2][:, (h % 2) * HEAD_DIM:(h % 2 + 1) * HEAD_DIM].astype(BF16) for h in range(N_Q_HEADS)]
    ks = [_rope(kp_ref[...], *(r[...] for r in cp_ref)).astype(BF16),
          _rope(kc_ref[...], *tq).astype(BF16),
          _rope(kn_ref[...], *(r[...] for r in cn_ref)).astype(BF16)]
    vs = [vp_ref[...].astype(BF16), vc_ref[...].astype(BF16), vn_ref[...].astype(BF16)]

    r = lax.broadcasted_iota(jnp.int32, (rows, BLOCK), 0) % BLOCK
    c = lax.broadcasted_iota(jnp.int32, (rows, BLOCK), 1)
    masks = [(c >= r) & (n > 0), None, (c <= r) & (n < nblk - 1)]

    outs = []
    for kv in range(N_KV_HEADS):
        q4 = jnp.concatenate(qh[kv * GQA_GROUP:(kv + 1) * GQA_GROUP], axis=0)
        lo, hi = kv * HEAD_DIM, (kv + 1) * HEAD_DIM
        s = []
        for blk in range(3):
            sb = lax.dot_general(q4, ks[blk][:, lo:hi], (((1,), (1,)), ((), ())), preferred_element_type=F32)
            if masks[blk] is not None:
                sb = jnp.where(masks[blk], sb, -jnp.inf)
            s.append(sb)
        sink = sink_ref[kv]
        m = jnp.maximum(jnp.maximum(jnp.max(s[0], axis=-1, keepdims=True), jnp.max(s[1], axis=-1, keepdims=True)),
                        jnp.maximum(jnp.max(s[2], axis=-1, keepdims=True), sink))
        e = [jnp.exp(sb - m) for sb in s]
        den = (jnp.sum(e[0], axis=-1, keepdims=True) + jnp.sum(e[1], axis=-1, keepdims=True)
               + jnp.sum(e[2], axis=-1, keepdims=True) + jnp.exp(sink - m))
        o4 = sum(jnp.dot(e[blk].astype(BF16), vs[blk][:, lo:hi], preferred_element_type=F32) for blk in range(3))
        o4 = o4 / den
        outs += [o4[g * BLOCK:(g + 1) * BLOCK, :] for g in range(GQA_GROUP)]
    o_ref[...] = jnp.concatenate(outs, axis=1).astype(o_ref.dtype)


def _attention(z2d, sink, nb, seq):
    nblk = seq // BLOCK
    tabs = _rope_tables(seq)
    sink_rows = jnp.repeat(sink.astype(F32).reshape(N_KV_HEADS, GQA_GROUP), BLOCK, axis=1)[:, :, None]
    kcol, vcol = OFF_K // LANES, OFF_V // LANES

    def prev(n):
        return jnp.maximum(n - 1, 0)

    def nxt(n):
        return jnp.minimum(n + 1, nblk - 1)

    def zspec(width, col, sel):
        return pl.BlockSpec((BLOCK, width), lambda b, n: (b * nblk + sel(n), col))

    def tspec(sel):
        return [pl.BlockSpec((BLOCK, LANES), lambda b, n: (sel(n), 0))] * 3

    same = lambda n: n
    return pl.pallas_call(
        _attn_body,
        grid=(nb, nblk),
        in_specs=[
            zspec(D_GROUP, OFF_Q // D_GROUP, same),
            zspec(LANES, kcol, prev), zspec(LANES, kcol, same), zspec(LANES, kcol, nxt),
            zspec(LANES, vcol, prev), zspec(LANES, vcol, same), zspec(LANES, vcol, nxt),
            tspec(same), tspec(prev), tspec(nxt),
            pl.BlockSpec((N_KV_HEADS, GQA_GROUP * BLOCK, 1), lambda b, n: (0, 0, 0)),
        ],
        out_specs=pl.BlockSpec((BLOCK, D_GROUP), lambda b, n: (b * nblk + n, 0)),
        out_shape=jax.ShapeDtypeStruct((nb * seq, D_GROUP), BF16),
        compiler_params=_params(("parallel", "parallel")),
        name="window_attention",
    )(z2d, z2d, z2d, z2d, z2d, z2d, z2d, list(tabs), list(tabs), list(tabs), sink_rows)


def _is(x, v):
    return abs(x - v) < 1e-12


def _scale(a, s):
    if a is None or _is(s, 0.0):
        return None
    if _is(s, 1.0):
        return a
    if _is(s, -1.0):
        return -a
    return a * s


def _cmul_const(a, w):
    ar, ai = a
    return (_psub(_scale(ar, w.real), _scale(ai, w.imag)), _padd(_scale(ar, w.imag), _scale(ai, w.real)))


def _padd(x, y):
    if x is None:
        return y
    if y is None:
        return x
    return x + y


def _psub(x, y):
    if y is None:
        return x
    if x is None:
        return -y
    return x - y


def _fft_blocks(xs, sign):
    n = len(xs)
    if n == 1:
        return list(xs)
    ev = _fft_blocks(xs[0::2], sign)
    od = _fft_blocks(xs[1::2], sign)
    out = [None] * n
    for k in range(n // 2):
        t = _cmul_const(od[k], np.exp(sign * 2j * math.pi * k / n))
        out[k] = (_padd(ev[k][0], t[0]), _padd(ev[k][1], t[1]))
        out[k + n // 2] = (_psub(ev[k][0], t[0]), _psub(ev[k][1], t[1]))
    return out


def _dense(a, like):
    return jnp.zeros_like(like) if a is None else a


def _dft_stack():
    k = np.arange(DFT_N, dtype=np.float64)
    ang = 2.0 * math.pi * np.outer(k, k) / DFT_N
    return jnp.asarray(np.concatenate([np.cos(ang), -np.sin(ang)], axis=0), F32)


def _twiddle_table(n1f):
    f1 = np.arange(n1f, dtype=np.float64)[:, None]
    t2 = np.arange(DFT_N, dtype=np.float64)[None, :]
    ang = (-2.0 * math.pi / (n1f * DFT_N)) * f1 * t2
    rep = lambda a: jnp.asarray(np.repeat(a.reshape(-1, 1), LANES, axis=1), F32)
    return rep(np.cos(ang)), rep(np.sin(ang))


def _block_dft(fst, qr, qi):
    rhs = jnp.concatenate([qr, qi], axis=1).astype(BF16)
    d = jnp.dot(fst, rhs, preferred_element_type=F32)
    return d[0:DFT_N, 0:LANES], d[0:DFT_N, LANES:], d[DFT_N:, 0:LANES], d[DFT_N:, LANES:]


def _forward_rows(xs, n1f):
    if len(xs) == n1f:
        return _fft_blocks(xs, -1.0)
    ev = _fft_blocks(xs, -1.0)
    od = _fft_blocks([_cmul_const(x, np.exp(-2j * math.pi * t / n1f)) for t, x in enumerate(xs)], -1.0)
    out = [None] * n1f
    out[0::2] = ev
    out[1::2] = od
    return out


def _spectrum_block(fst, q_scr, twr_ref, twi_ref, f):
    rows = pl.ds(pl.multiple_of(f * DFT_N, DFT_N), DFT_N)
    cr, ci, sr, si = _block_dft(fst, q_scr[rows, 0:LANES], q_scr[rows, LANES:])
    return rows, cr - si, ci + sr


def _store_twiddled(q_scr, twr_ref, twi_ref, blocks, r, like):
    for f, (pr, pi) in enumerate(blocks):
        rows = pl.ds(f * DFT_N + r, SUBLANES)
        pr, pi = _dense(pr, like), _dense(pi, like)
        if f == 0:
            qr, qi = pr, pi
        else:
            tr, ti = twr_ref[rows, :], twi_ref[rows, :]
            qr, qi = pr * tr - pi * ti, pr * ti + pi * tr
        q_scr[rows, 0:LANES] = qr
        q_scr[rows, LANES:] = qi


def _hconv_body(u_ref, g_ref, kr_ref, ki_ref, twr_ref, twi_ref, fst_ref, b_ref, o_ref, q_scr, *, seq):
    n1 = seq // DFT_N
    n1f = 2 * n1
    fst = fst_ref[...]

    def fwd(i, carry):
        r = pl.multiple_of(i * SUBLANES, SUBLANES)
        xs = [(u_ref[pl.ds(t * DFT_N + r, SUBLANES), :], u_ref[pl.ds(seq + t * DFT_N + r, SUBLANES), :])
              for t in range(n1)]
        _store_twiddled(q_scr, twr_ref, twi_ref, _forward_rows(xs, n1f), r, xs[0][0])
        return carry

    lax.fori_loop(0, DFT_N // SUBLANES, fwd, 0)

    def mid(f, carry):
        rows, xr, xi = _spectrum_block(fst, q_scr, twr_ref, twi_ref, f)
        kr, ki = kr_ref[rows, :], ki_ref[rows, :]
        cr, ci, sr, si = _block_dft(fst, xr * kr - xi * ki, xr * ki + xi * kr)
        rr, ri = cr + si, ci - sr
        tr, ti = twr_ref[rows, :], twi_ref[rows, :]
        q_scr[rows, 0:LANES] = rr * tr + ri * ti
        q_scr[rows, LANES:] = ri * tr - rr * ti
        return carry

    lax.fori_loop(0, n1f, mid, 0)

    bias = b_ref[...]

    def inv(i, carry):
        r = pl.multiple_of(i * SUBLANES, SUBLANES)
        ss = [(q_scr[pl.ds(f * DFT_N + r, SUBLANES), 0:LANES], q_scr[pl.ds(f * DFT_N + r, SUBLANES), LANES:])
              for f in range(n1f)]
        ev = _fft_blocks(ss[0::2], 1.0)
        od = _fft_blocks(ss[1::2], 1.0)
        for t in range(n1):
            o = _cmul_const(od[t], np.exp(2j * math.pi * t / n1f))
            for b, y in enumerate((ev[t][0] + o[0], ev[t][1] + o[1])):
                rows = pl.ds(b * seq + t * DFT_N + r, SUBLANES)
                o_ref[rows, :] = (g_ref[rows, :] * (y + bias * u_ref[rows, :])).astype(o_ref.dtype)
        return carry

    lax.fori_loop(0, DFT_N // SUBLANES, inv, 0)


def _hconv(u, ucol, g, gcol, spec_r, spec_i, order, tw, fst, bias, nb, seq, out_dtype):
    n1f = 2 * seq // DFT_N
    nc = D_GROUP // LANES
    once = pl.Buffered(1)
    return pl.pallas_call(
        functools.partial(_hconv_body, seq=seq),
        grid=(nc, nb // 2),
        in_specs=[
            pl.BlockSpec((2 * seq, LANES), lambda c, p: (p, ucol + c)),
            pl.BlockSpec((2 * seq, LANES), lambda c, p: (p, gcol + c)),
            pl.BlockSpec((None, n1f * DFT_N, LANES), lambda c, p: (order, 0, c), pipeline_mode=once),
            pl.BlockSpec((None, n1f * DFT_N, LANES), lambda c, p: (order, 0, c), pipeline_mode=once),
            pl.BlockSpec((n1f * DFT_N, LANES), lambda c, p: (0, 0), pipeline_mode=once),
            pl.BlockSpec((n1f * DFT_N, LANES), lambda c, p: (0, 0), pipeline_mode=once),
            pl.BlockSpec((2 * DFT_N, DFT_N), lambda c, p: (0, 0)),
            pl.BlockSpec((1, LANES), lambda c, p: (0, c)),
        ],
        out_specs=pl.BlockSpec((2 * seq, LANES), lambda c, p: (p, c)),
        out_shape=jax.ShapeDtypeStruct((nb * seq, D_GROUP), out_dtype),
        scratch_shapes=[pltpu.VMEM((n1f * DFT_N, 2 * LANES), F32)],
        compiler_params=_params(("arbitrary", "arbitrary")),
        name="hyena_long_conv",
    )(u, g, spec_r, spec_i, tw[0], tw[1], fst, bias[order][None, :])


def _hidden_body(f_ref, w1_ref, b1_ref, w2_ref, b2_ref, fr_ref, o_ref):
    hi = lax.Precision.HIGHEST
    fr = fr_ref[...]
    h = jnp.sin(fr * (jnp.dot(f_ref[...], w1_ref[...], precision=hi, preferred_element_type=F32) + b1_ref[...]))
    o_ref[...] = jnp.sin(fr * (jnp.dot(h, w2_ref[...], precision=hi, preferred_element_type=F32) + b2_ref[...]))


def _filter_features(seq):
    bands = np.linspace(1e-4, HY_BANDS - 1, HY_BANDS)[None, :]

    def feats(j):
        t = j / (seq - 1.0)
        wpos = (2.0 * math.pi / seq) * j
        return np.concatenate([t, np.cos(bands * wpos), -np.sin(bands * wpos)], axis=-1), t

    j = np.arange(seq, dtype=np.float64)[:, None]
    ff, tf = feats(j)
    fb, tb = feats(seq - j)
    return jnp.asarray(np.concatenate([ff, fb], axis=1), F32), jnp.asarray(np.concatenate([tf, tb], axis=1), F32)


def _blockdiag2(w):
    z = jnp.zeros_like(w)
    return jnp.concatenate([jnp.concatenate([w, z], axis=1), jnp.concatenate([z, w], axis=1)], axis=0)


def _filter_hidden(feats, w1, b1, w2, b2, freq):
    seq = feats.shape[0]
    two = lambda v: jnp.concatenate([v, v])[None, :]
    tr = 512
    full = lambda a: pl.BlockSpec(a.shape, lambda i: (0, 0))
    args = (_blockdiag2(w1), two(b1), _blockdiag2(w2), two(b2), two(freq))
    return pl.pallas_call(
        _hidden_body,
        grid=(seq // tr,),
        in_specs=[pl.BlockSpec((tr, 2 * HY_EMB), lambda i: (i, 0))] + [full(a) for a in args],
        out_specs=pl.BlockSpec((tr, 2 * HY_HIDDEN), lambda i: (i, 0)),
        out_shape=jax.ShapeDtypeStruct((seq, 2 * HY_HIDDEN), F32),
        compiler_params=_params(("parallel",)),
        name="hyena_filter_hidden",
    )(feats, *args)


def _hfilter_body(h_ref, t_ref, w3f_ref, w3b_ref, df_ref, db_ref, twr_ref, twi_ref, fst_ref, or_ref, oi_ref,
                  k_scr, q_scr, *, seq):
    n1f = 2 * seq // DFT_N
    hi = lax.Precision.HIGHEST
    h = h_ref[...]
    t = t_ref[...]
    row = lax.broadcasted_iota(jnp.int32, (seq, 1), 0)
    kf = jnp.dot(h, w3f_ref[...], precision=hi, preferred_element_type=F32) * jnp.exp(-t[:, 0:1] * jnp.abs(df_ref[...]))
    kb = jnp.dot(h, w3b_ref[...], precision=hi, preferred_element_type=F32) * jnp.exp(-t[:, 1:2] * jnp.abs(db_ref[...]))
    kb = jnp.where(row > 0, kb, 0.0)
    norm = jnp.sum(jnp.abs(kf), axis=0, keepdims=True) + jnp.sum(jnp.abs(kb), axis=0, keepdims=True)
    scale = (1.0 / (2 * seq)) / norm
    k_scr[0:seq, :] = kf * scale
    k_scr[seq:, :] = kb * scale
    fst = fst_ref[...]

    def fwd(i, carry):
        r = pl.multiple_of(i * SUBLANES, SUBLANES)
        xs = [(k_scr[pl.ds(t1 * DFT_N + r, SUBLANES), :], None) for t1 in range(n1f)]
        _store_twiddled(q_scr, twr_ref, twi_ref, _forward_rows(xs, n1f), r, xs[0][0])
        return carry

    lax.fori_loop(0, DFT_N // SUBLANES, fwd, 0)

    def mid(f, carry):
        rows, xr, xi = _spectrum_block(fst, q_scr, twr_ref, twi_ref, f)
        or_ref[rows, :] = xr
        oi_ref[rows, :] = xi
        return carry

    lax.fori_loop(0, n1f, mid, 0)


def _hfilter(hid, times, w3, decay, tw, fst, seq):
    n1f = 2 * seq // DFT_N
    nc = D_GROUP // LANES
    zeros = jnp.zeros((HY_HIDDEN, w3.shape[1]), w3.dtype)
    w3f = jnp.concatenate([w3, zeros], axis=0)
    w3b = jnp.concatenate([zeros, w3], axis=0)
    dec = decay[None, :]
    once = pl.Buffered(1)
    fcol = lambda o, c: (0, 2 * nc * o + c)
    bcol = lambda o, c: (0, 2 * nc * o + nc + c)
    out_spec = pl.BlockSpec((None, n1f * DFT_N, LANES), lambda o, c: (o, 0, c))
    out_shape = jax.ShapeDtypeStruct((HY_ORDER, n1f * DFT_N, D_GROUP), F32)
    return pl.pallas_call(
        functools.partial(_hfilter_body, seq=seq),
        grid=(HY_ORDER, nc),
        in_specs=[
            pl.BlockSpec((seq, 2 * HY_HIDDEN), lambda o, c: (0, 0)),
            pl.BlockSpec((seq, 2), lambda o, c: (0, 0)),
            pl.BlockSpec((2 * HY_HIDDEN, LANES), fcol),
            pl.BlockSpec((2 * HY_HIDDEN, LANES), bcol),
            pl.BlockSpec((1, LANES), fcol),
            pl.BlockSpec((1, LANES), bcol),
            pl.BlockSpec((n1f * DFT_N, LANES), lambda o, c: (0, 0), pipeline_mode=once),
            pl.BlockSpec((n1f * DFT_N, LANES), lambda o, c: (0, 0), pipeline_mode=once),
            pl.BlockSpec((2 * DFT_N, DFT_N), lambda o, c: (0, 0)),
        ],
        out_specs=[out_spec, out_spec],
        out_shape=[out_shape, out_shape],
        scratch_shapes=[pltpu.VMEM((2 * seq, LANES), F32), pltpu.VMEM((n1f * DFT_N, 2 * LANES), F32)],
        compiler_params=_params(("arbitrary", "arbitrary")),
        name="hyena_filter_spectrum",
    )(hid, times, w3f, w3b, dec, dec, tw[0], tw[1], fst)


def _trunk(x, p, fst):
    nb, seq, _ = x.shape
    x2d = x.reshape(nb * seq, D_MODEL)
    tw = _twiddle_table(2 * seq // DFT_N)
    feats, times = _filter_features(seq)
    vcol = 0
    g1col = D_GROUP // LANES
    g2col = 2 * D_GROUP // LANES
    for l in range(DEPTH):
        z = _inproj(x2d, p["mix_norm_g"][l][None, :], p["w_in"][l])
        out_a = _pool(z, p["pool_w"][l], p["pool_scale"][l][None, :], nb, seq)
        out_b = _attention(z, p["attn_sink"][l], nb, seq)
        out_c = _sconv(z, p["sconv_w"][l], nb, seq)
        hz = _hshort(z, p["hy_short_w"][l], nb, seq)
        hid = _filter_hidden(feats, p["hy_w1"][l], p["hy_b1"][l], p["hy_w2"][l], p["hy_b2"][l], p["hy_freq"][l])
        spec_r, spec_i = _hfilter(hid, times, p["hy_w3"][l], p["hy_decay"][l], tw, fst, seq)
        bias = p["hy_bias"][l]
        u1 = _hconv(hz, vcol, hz, g1col, spec_r, spec_i, 0, tw, fst, bias, nb, seq, F32)
        out_d = _hconv(u1, 0, hz, g2col, spec_r, spec_i, 1, tw, fst, bias, nb, seq, BF16)
        x2d = _outproj(x2d, (out_a, out_b, out_c, out_d), p["w_out"][l])
        x2d = _ffn(x2d, p["ffn_norm_g"][l][None, :], p["w_gate_up"][l], p["w_down"][l],
                   p["final_norm_g"][None, :], final_norm=(l == DEPTH - 1))
    return x2d.reshape(nb, seq, D_MODEL)


def kernel(x_prompt, x_sample, mix_norm_g, w_in, pool_w, pool_scale, attn_sink, sconv_w, hy_short_w, hy_w1, hy_b1, hy_w2, hy_b2, hy_w3, hy_freq, hy_decay, hy_bias, w_out, ffn_norm_g, w_gate_up, w_down, final_norm_g):
    p = dict(
        mix_norm_g=mix_norm_g, w_in=w_in.astype(BF16), pool_w=pool_w.astype(BF16), pool_scale=pool_scale,
        attn_sink=attn_sink, sconv_w=sconv_w, hy_short_w=hy_short_w, hy_w1=hy_w1, hy_b1=hy_b1, hy_w2=hy_w2,
        hy_b2=hy_b2, hy_w3=hy_w3, hy_freq=hy_freq, hy_decay=hy_decay, hy_bias=hy_bias,
        w_out=w_out.astype(BF16), ffn_norm_g=ffn_norm_g, w_gate_up=w_gate_up.astype(BF16),
        w_down=w_down.astype(BF16), final_norm_g=final_norm_g,
    )
    fst = _dft_stack().astype(BF16)
    return (_trunk(x_prompt, p, fst), _trunk(x_sample, p, fst))
```

```python
import functools
import math

import numpy as np
import jax
import jax.numpy as jnp
from jax import lax
from jax.experimental import pallas as pl
from jax.experimental.pallas import tpu as pltpu

F32 = jnp.float32
BF16 = jnp.bfloat16

D_MODEL = 2048
DEPTH = 2
D_GROUP = 512
POOL_WINDOWS = (2, 4, 8, 16)
POOL_CH = 128
HEAD_DIM = 64
N_Q_HEADS = 8
N_KV_HEADS = 2
GQA_GROUP = 4
BLOCK = 128
ROPE_DIM = 16
ROPE_THETA = 500000.0
HY_ORDER = 2
HY_EMB = 33
HY_BANDS = 16
HY_HIDDEN = 64
D_FF = 5632
NORM_EPS = 1e-6
OFF_Q = 512
OFF_K = 1024
OFF_V = 1152
OFF_C = 1280
OFF_D = 2816
D_IN = 4352

LANES = 128
SUBLANES = 8
DFT_N = 256
VMEM_LIMIT = 56 * 2 ** 20


def _params(sem, vmem=VMEM_LIMIT):
    return pltpu.CompilerParams(dimension_semantics=sem, vmem_limit_bytes=vmem)


def _rms(x, g):
    ms = jnp.mean(x * x, axis=-1, keepdims=True)
    return x * lax.rsqrt(ms + NORM_EPS) * g


def _inproj_body(x_ref, g_ref, w_ref, o_ref):
    hn = _rms(x_ref[...], g_ref[...]).astype(BF16)
    o_ref[...] = jnp.dot(hn, w_ref[...], preferred_element_type=F32)


def _inproj(x2d, g, w, layer):
    T = x2d.shape[0]
    tm = 512
    return pl.pallas_call(
        _inproj_body,
        grid=(T // tm,),
        in_specs=[
            pl.BlockSpec((tm, D_MODEL), lambda i: (i, 0)),
            pl.BlockSpec((1, D_MODEL), lambda i: (0, 0)),
            pl.BlockSpec((None, D_MODEL, D_IN), lambda i: (layer, 0, 0), pipeline_mode=pl.Buffered(1)),
        ],
        out_specs=pl.BlockSpec((tm, D_IN), lambda i: (i, 0)),
        out_shape=jax.ShapeDtypeStruct((T, D_IN), F32),
        compiler_params=_params(("parallel",)),
        name="inproj",
    )(x2d, g, w)


def _outproj_body(x_ref, a_ref, b_ref, c_ref, d_ref, w_ref, o_ref):
    acc = x_ref[...]
    for k, m_ref in enumerate((a_ref, b_ref, c_ref, d_ref)):
        acc = acc + jnp.dot(m_ref[...], w_ref[k * D_GROUP:(k + 1) * D_GROUP, :], preferred_element_type=F32)
    o_ref[...] = acc


def _outproj(x2d, mixes, w, layer):
    T = x2d.shape[0]
    tm = 512
    mix_spec = pl.BlockSpec((tm, D_GROUP), lambda i: (i, 0))
    return pl.pallas_call(
        _outproj_body,
        grid=(T // tm,),
        in_specs=[pl.BlockSpec((tm, D_MODEL), lambda i: (i, 0)), mix_spec, mix_spec, mix_spec, mix_spec,
                  pl.BlockSpec((None, D_MODEL, D_MODEL), lambda i: (layer, 0, 0), pipeline_mode=pl.Buffered(1))],
        out_specs=pl.BlockSpec((tm, D_MODEL), lambda i: (i, 0)),
        out_shape=jax.ShapeDtypeStruct((T, D_MODEL), F32),
        compiler_params=_params(("parallel",)),
        name="outproj",
    )(x2d, *mixes, w)


def _ffn_body(x_ref, g_ref, wg_ref, wu_ref, wd_ref, gf_ref, o_ref, h_ref, *, final_norm):
    j = pl.program_id(1)

    @pl.when(j == 0)
    def _():
        x = x_ref[...]
        h_ref[...] = _rms(x, g_ref[...]).astype(BF16)
        o_ref[...] = x

    h = h_ref[...]
    gate = jnp.dot(h, wg_ref[...], preferred_element_type=F32)
    up = jnp.dot(h, wu_ref[...], preferred_element_type=F32)
    act = (gate * jax.nn.sigmoid(gate) * up).astype(BF16)
    o_ref[...] += jnp.dot(act, wd_ref[...], preferred_element_type=F32)

    if final_norm:
        @pl.when(j == pl.num_programs(1) - 1)
        def _():
            o_ref[...] = _rms(o_ref[...], gf_ref[...])


def _ffn(x2d, g, w_gate_up, w_down, gf, layer, final_norm):
    T = x2d.shape[0]
    tm, tf = 1024, 512
    nf = D_FF // tf
    return pl.pallas_call(
        functools.partial(_ffn_body, final_norm=final_norm),
        grid=(T // tm, nf),
        in_specs=[
            pl.BlockSpec((tm, D_MODEL), lambda i, j: (i, 0)),
            pl.BlockSpec((1, D_MODEL), lambda i, j: (0, 0)),
            pl.BlockSpec((None, D_MODEL, tf), lambda i, j: (layer, 0, j)),
            pl.BlockSpec((None, D_MODEL, tf), lambda i, j: (layer, 0, j + nf)),
            pl.BlockSpec((None, tf, D_MODEL), lambda i, j: (layer, j, 0)),
            pl.BlockSpec((1, D_MODEL), lambda i, j: (0, 0)),
        ],
        out_specs=pl.BlockSpec((tm, D_MODEL), lambda i, j: (i, 0)),
        out_shape=jax.ShapeDtypeStruct((T, D_MODEL), F32),
        scratch_shapes=[pltpu.VMEM((tm, D_MODEL), BF16)],
        compiler_params=_params(("parallel", "arbitrary")),
        name="ffn",
    )(x2d, g, w_gate_up, w_gate_up, w_down, gf)


def _shift_rows(x, s):
    n = x.shape[0]
    r = pltpu.roll(x, s % n, axis=0)
    t = lax.broadcasted_iota(jnp.int32, x.shape, 0)
    keep = (t >= s) if s > 0 else (t < n + s)
    return jnp.where(keep, r, 0.0)


def _conv3(x, w_ref):
    return _shift_rows(x, 1) * w_ref[0:1, :] + x * w_ref[1:2, :] + _shift_rows(x, -1) * w_ref[2:3, :]


POOL_PAD = 16


def _pool_body(p_ref, w_ref, s_ref, o_ref, pad_ref, *, seq):
    c = pl.program_id(1)
    p = p_ref[...]
    zeros = jnp.zeros((POOL_PAD, LANES), F32)
    pad_ref[0:POOL_PAD, :] = zeros
    pad_ref[POOL_PAD + seq:2 * POOL_PAD + seq, :] = zeros
    pad_ref[POOL_PAD:POOL_PAD + seq, :] = p
    n = seq + 2 * POOL_PAD
    t = lax.broadcasted_iota(jnp.int32, (seq, 1), 0)

    for k, win in enumerate(POOL_WINDOWS):
        @pl.when(c == k)
        def _(win=win):
            s = pad_ref[...]
            s = s + pltpu.roll(s, 1, axis=0)
            step = 1
            while 2 * step < win:
                s = pltpu.roll(s, step, axis=0) + pltpu.roll(s, n - step, axis=0)
                step *= 2
            s = s[POOL_PAD:POOL_PAD + seq, :]
            half = win // 2
            cnt = (jnp.minimum(t + half, seq) - jnp.maximum(t - half, 0)).astype(F32)
            d = s / cnt - p
            out = jnp.dot(d.astype(BF16), w_ref[...], preferred_element_type=F32)
            o_ref[...] = (out * s_ref[...]).astype(o_ref.dtype)


def _pool(z2d, pool_w, pool_scale, layer, nb, seq):
    return pl.pallas_call(
        functools.partial(_pool_body, seq=seq),
        grid=(nb, len(POOL_WINDOWS)),
        in_specs=[
            pl.BlockSpec((seq, POOL_CH), lambda b, c: (b, c)),
            pl.BlockSpec((None, None, POOL_CH, POOL_CH), lambda b, c: (layer, c, 0, 0)),
            pl.BlockSpec((1, POOL_CH), lambda b, c: (0, c)),
        ],
        out_specs=pl.BlockSpec((seq, POOL_CH), lambda b, c: (b, c)),
        out_shape=jax.ShapeDtypeStruct((nb * seq, D_GROUP), BF16),
        scratch_shapes=[pltpu.VMEM((seq + 2 * POOL_PAD, LANES), F32)],
        compiler_params=_params(("parallel", "parallel")),
        name="pool_mixer",
    )(z2d, pool_w, pool_scale)


def _sconv_body(h_ref, b_ref, c_ref, w_ref, o_ref):
    o_ref[...] = (b_ref[...] * _conv3(c_ref[...] * h_ref[...], w_ref)).astype(o_ref.dtype)


def _sconv(z2d, w, nb, seq):
    nc = D_GROUP // LANES
    c0 = OFF_C // LANES
    return pl.pallas_call(
        _sconv_body,
        grid=(nb, nc),
        in_specs=[
            pl.BlockSpec((seq, LANES), lambda b, c: (b, c0 + c)),
            pl.BlockSpec((seq, LANES), lambda b, c: (b, c0 + nc + c)),
            pl.BlockSpec((seq, LANES), lambda b, c: (b, c0 + 2 * nc + c)),
            pl.BlockSpec((3, LANES), lambda b, c: (0, c)),
        ],
        out_specs=pl.BlockSpec((seq, LANES), lambda b, c: (b, c)),
        out_shape=jax.ShapeDtypeStruct((nb * seq, D_GROUP), BF16),
        compiler_params=_params(("parallel", "parallel")),
        name="gated_sconv",
    )(z2d, z2d, z2d, w)


def _hshort_body(z_ref, w_ref, o_ref):
    o_ref[...] = _conv3(z_ref[...], w_ref)


def _hshort(z2d, w, nb, seq):
    nc = 3 * D_GROUP // LANES
    c0 = OFF_D // LANES
    return pl.pallas_call(
        _hshort_body,
        grid=(nb, nc),
        in_specs=[pl.BlockSpec((seq, LANES), lambda b, c: (b, c0 + c)),
                  pl.BlockSpec((3, LANES), lambda b, c: (0, c))],
        out_specs=pl.BlockSpec((seq, LANES), lambda b, c: (b, c)),
        out_shape=jax.ShapeDtypeStruct((nb * seq, 3 * D_GROUP), F32),
        compiler_params=_params(("parallel", "parallel")),
        name="hyena_short_conv",
    )(z2d, w)


def _rope_tables(seq):
    inv_freq = ROPE_THETA ** (-np.arange(0, ROPE_DIM, 2, dtype=np.float64) / ROPE_DIM)
    ang = np.arange(seq, dtype=np.float64)[:, None] * inv_freq[None, :]
    half = ROPE_DIM // 2
    cos = np.ones((seq, HEAD_DIM))
    s_lo = np.zeros((seq, HEAD_DIM))
    s_hi = np.zeros((seq, HEAD_DIM))
    cos[:, :half] = np.cos(ang)
    cos[:, half:ROPE_DIM] = np.cos(ang)
    s_lo[:, half:ROPE_DIM] = np.sin(ang)
    s_hi[:, :half] = -np.sin(ang)
    rep = LANES // HEAD_DIM
    return tuple(jnp.asarray(np.tile(a, (1, rep)), F32) for a in (cos, s_lo, s_hi))


def _rope(x, cos, s_lo, s_hi):
    half = ROPE_DIM // 2
    return x * cos + pltpu.roll(x, half, axis=1) * s_lo + pltpu.roll(x, LANES - half, axis=1) * s_hi


def _attn_body(q_ref, kp_ref, kc_ref, kn_ref, vp_ref, vc_ref, vn_ref, cq_ref, cp_ref, cn_ref, sink_ref, o_ref):
    n = pl.program_id(1)
    nblk = pl.num_programs(1)
    tq = tuple(r[...] for r in cq_ref)
    tabs = [tuple(r[...] for r in cp_ref), tq, tuple(r[...] for r in cn_ref)]
    two = 2 * BLOCK

    q = q_ref[...]
    qg = [(_rope(q[:, g * LANES:(g + 1) * LANES], *tq) * (HEAD_DIM ** -0.5)).astype(BF16)
          for g in range(D_GROUP // LANES)]

    lane = lax.broadcasted_iota(jnp.int32, (BLOCK, LANES), 1)
    first = lane < HEAD_DIM
    half_ones = jnp.where(first, 1.0, 0.0)
    ones_bd = jnp.concatenate([half_ones, 1.0 - half_ones], axis=0)

    def per_head(x):
        xr = pltpu.roll(x, HEAD_DIM, axis=1)
        return [jnp.concatenate([jnp.where(first, x, 0.0), jnp.where(first, 0.0, xr)], axis=0),
                jnp.concatenate([jnp.where(first, xr, 0.0), jnp.where(first, 0.0, x)], axis=0)]

    kcat, vcat = [], []
    for k_ref, v_ref, tab in zip((kp_ref, kc_ref, kn_ref), (vp_ref, vc_ref, vn_ref), tabs):
        kcat.append([a.astype(BF16) for a in per_head(_rope(k_ref[...], *tab))])
        vcat.append([jnp.concatenate([a, ones_bd], axis=1).astype(BF16) for a in per_head(v_ref[...])])

    r = lax.broadcasted_iota(jnp.int32, (two, two), 0) % BLOCK
    c = lax.broadcasted_iota(jnp.int32, (two, two), 1)
    left = c < BLOCK
    c = c % BLOCK
    masks = [(c >= r) & (n > 0), None, (c <= r) & (n < nblk - 1)]
    first2 = lax.broadcasted_iota(jnp.int32, (two, LANES), 1) < HEAD_DIM

    outs = []
    for kv in range(N_KV_HEADS):
        qs = jnp.concatenate([qg[2 * kv], qg[2 * kv + 1]], axis=0)
        s = []
        for blk in range(3):
            sb = lax.dot_general(qs, kcat[blk][kv], (((1,), (1,)), ((), ())), preferred_element_type=F32)
            if masks[blk] is not None:
                sb = jnp.where(masks[blk], sb, -jnp.inf)
            s.append(sb)
        sink = sink_ref[kv]
        top = jnp.maximum(jnp.maximum(s[0], s[1]), s[2])
        m = jnp.where(left, jnp.max(top[:, :BLOCK], axis=-1, keepdims=True),
                      jnp.max(top[:, BLOCK:], axis=-1, keepdims=True))
        m = jnp.maximum(m, sink)
        acc = sum(jnp.dot(jnp.exp(s[blk] - m).astype(BF16), vcat[blk][kv], preferred_element_type=F32)
                  for blk in range(3))
        m_out = jnp.where(first2, m[:, :LANES], m[:, LANES:])
        sink_out = jnp.where(first2, sink[:, :LANES], sink[:, LANES:])
        o = acc[:, :LANES] / (acc[:, LANES:] + jnp.exp(sink_out - m_out))
        outs += [o[:BLOCK], o[BLOCK:]]
    o_ref[...] = jnp.concatenate(outs, axis=1).astype(o_ref.dtype)


def _attention(z2d, sink, nb, seq):
    nblk = seq // BLOCK
    tabs = _rope_tables(seq)
    sink_tab = jnp.repeat(jnp.repeat(sink.astype(F32).reshape(N_KV_HEADS, 2, 2), BLOCK, axis=1), BLOCK, axis=2)
    kcol, vcol = OFF_K // LANES, OFF_V // LANES

    def prev(n):
        return jnp.maximum(n - 1, 0)

    def nxt(n):
        return jnp.minimum(n + 1, nblk - 1)

    def zspec(width, col, sel):
        return pl.BlockSpec((BLOCK, width), lambda b, n: (b * nblk + sel(n), col))

    def tspec(sel):
        return [pl.BlockSpec((BLOCK, LANES), lambda b, n: (sel(n), 0))] * 3

    same = lambda n: n
    return pl.pallas_call(
        _attn_body,
        grid=(nb, nblk),
        in_specs=[
            zspec(D_GROUP, OFF_Q // D_GROUP, same),
            zspec(LANES, kcol, prev), zspec(LANES, kcol, same), zspec(LANES, kcol, nxt),
            zspec(LANES, vcol, prev), zspec(LANES, vcol, same), zspec(LANES, vcol, nxt),
            tspec(same), tspec(prev), tspec(nxt),
            pl.BlockSpec((N_KV_HEADS, 2 * BLOCK, 2 * BLOCK), lambda b, n: (0, 0, 0)),
        ],
        out_specs=pl.BlockSpec((BLOCK, D_GROUP), lambda b, n: (b * nblk + n, 0)),
        out_shape=jax.ShapeDtypeStruct((nb * seq, D_GROUP), BF16),
        compiler_params=_params(("parallel", "parallel")),
        name="window_attention",
    )(z2d, z2d, z2d, z2d, z2d, z2d, z2d, list(tabs), list(tabs), list(tabs), sink_tab)


def _is(x, v):
    return abs(x - v) < 1e-12


def _scale(a, s):
    if a is None or _is(s, 0.0):
        return None
    if _is(s, 1.0):
        return a
    if _is(s, -1.0):
        return -a
    return a * s


def _cmul_const(a, w):
    ar, ai = a
    return (_psub(_scale(ar, w.real), _scale(ai, w.imag)), _padd(_scale(ar, w.imag), _scale(ai, w.real)))


def _padd(x, y):
    if x is None:
        return y
    if y is None:
        return x
    return x + y


def _psub(x, y):
    if y is None:
        return x
    if x is None:
        return -y
    return x - y


def _fft_blocks(xs, sign):
    n = len(xs)
    if n == 1:
        return list(xs)
    ev = _fft_blocks(xs[0::2], sign)
    od = _fft_blocks(xs[1::2], sign)
    out = [None] * n
    for k in range(n // 2):
        t = _cmul_const(od[k], np.exp(sign * 2j * math.pi * k / n))
        out[k] = (_padd(ev[k][0], t[0]), _padd(ev[k][1], t[1]))
        out[k + n // 2] = (_psub(ev[k][0], t[0]), _psub(ev[k][1], t[1]))
    return out


def _dense(a, like):
    return jnp.zeros_like(like) if a is None else a


def _dft_stack():
    k = np.arange(DFT_N, dtype=np.float64)
    ang = 2.0 * math.pi * np.outer(k, k) / DFT_N
    return jnp.asarray(np.concatenate([np.cos(ang), -np.sin(ang)], axis=0), F32)


def _twiddle_table(n1f):
    f1 = np.arange(n1f, dtype=np.float64)[:, None]
    t2 = np.arange(DFT_N, dtype=np.float64)[None, :]
    ang = (-2.0 * math.pi / (n1f * DFT_N)) * f1 * t2
    rep = lambda a: jnp.asarray(np.repeat(a.reshape(-1, 1), LANES, axis=1), F32)
    return rep(np.cos(ang)), rep(np.sin(ang))


def _block_dft(fst, qr, qi):
    rhs = jnp.concatenate([qr, qi], axis=1).astype(BF16)
    d = jnp.dot(fst, rhs, preferred_element_type=F32)
    return d[0:DFT_N, 0:LANES], d[0:DFT_N, LANES:], d[DFT_N:, 0:LANES], d[DFT_N:, LANES:]


def _forward_rows(xs, n1f):
    if len(xs) == n1f:
        return _fft_blocks(xs, -1.0)
    ev = _fft_blocks(xs, -1.0)
    od = _fft_blocks([_cmul_const(x, np.exp(-2j * math.pi * t / n1f)) for t, x in enumerate(xs)], -1.0)
    out = [None] * n1f
    out[0::2] = ev
    out[1::2] = od
    return out


MID_BLOCKS = 4


def _block_rows(f):
    return pl.ds(pl.multiple_of(f * DFT_N, DFT_N), DFT_N)


def _store_rows(q_scr, blocks, r, like):
    for f, (pr, pi) in enumerate(blocks):
        rows = pl.ds(f * DFT_N + r, SUBLANES)
        q_scr[rows, 0:LANES] = _dense(pr, like)
        q_scr[rows, LANES:] = _dense(pi, like)


def _twiddled_block_dft(fst, q_scr, twr_ref, twi_ref, rows):
    pr, pi = q_scr[rows, 0:LANES], q_scr[rows, LANES:]
    tr, ti = twr_ref[rows, :], twi_ref[rows, :]
    cr, ci, sr, si = _block_dft(fst, pr * tr - pi * ti, pr * ti + pi * tr)
    return cr - si, ci + sr


def _hconv_body(u_ref, g_ref, kr_ref, ki_ref, twr_ref, twi_ref, fst_ref, b_ref, o_ref, q_scr, *, seq):
    n1 = seq // DFT_N
    n1f = 2 * n1
    fst = fst_ref[...]

    def fwd(i, carry):
        r = pl.multiple_of(i * SUBLANES, SUBLANES)
        xs = [(u_ref[pl.ds(t * DFT_N + r, SUBLANES), :], u_ref[pl.ds(seq + t * DFT_N + r, SUBLANES), :])
              for t in range(n1)]
        _store_rows(q_scr, _forward_rows(xs, n1f), r, xs[0][0])
        return carry

    lax.fori_loop(0, DFT_N // SUBLANES, fwd, 0)

    def mid(i, carry):
        rows = [_block_rows(i * MID_BLOCKS + k) for k in range(MID_BLOCKS)]
        x = [_twiddled_block_dft(fst, q_scr, twr_ref, twi_ref, r) for r in rows]
        y = []
        for r, (xr, xi) in zip(rows, x):
            kr, ki = kr_ref[r, :], ki_ref[r, :]
            y.append((xr * kr - xi * ki, xr * ki + xi * kr))
        e = [_block_dft(fst, yr, yi) for yr, yi in y]
        s = []
        for r, (cr, ci, sr, si) in zip(rows, e):
            rr, ri = cr + si, ci - sr
            tr, ti = twr_ref[r, :], twi_ref[r, :]
            s.append((rr * tr + ri * ti, ri * tr - rr * ti))
        for r, (sr_, si_) in zip(rows, s):
            q_scr[r, 0:LANES] = sr_
            q_scr[r, LANES:] = si_
        return carry

    lax.fori_loop(0, n1f // MID_BLOCKS, mid, 0)

    bias = b_ref[...]

    def inv(i, carry):
        r = pl.multiple_of(i * SUBLANES, SUBLANES)
        ss = [(q_scr[pl.ds(f * DFT_N + r, SUBLANES), 0:LANES], q_scr[pl.ds(f * DFT_N + r, SUBLANES), LANES:])
              for f in range(n1f)]
        ev = _fft_blocks(ss[0::2], 1.0)
        od = _fft_blocks(ss[1::2], 1.0)
        for t in range(n1):
            o = _cmul_const(od[t], np.exp(2j * math.pi * t / n1f))
            for b, y in enumerate((ev[t][0] + o[0], ev[t][1] + o[1])):
                rows = pl.ds(b * seq + t * DFT_N + r, SUBLANES)
                o_ref[rows, :] = (g_ref[rows, :] * (y + bias * u_ref[rows, :])).astype(o_ref.dtype)
        return carry

    lax.fori_loop(0, DFT_N // SUBLANES, inv, 0)


def _hconv(u, ucol, g, gcol, spec_r, spec_i, order, tw, fst, bias, nb, seq, out_dtype):
    n1f = 2 * seq // DFT_N
    nc = D_GROUP // LANES
    once = pl.Buffered(1)
    return pl.pallas_call(
        functools.partial(_hconv_body, seq=seq),
        grid=(nc, nb // 2),
        in_specs=[
            pl.BlockSpec((2 * seq, LANES), lambda c, p: (p, ucol + c)),
            pl.BlockSpec((2 * seq, LANES), lambda c, p: (p, gcol + c)),
            pl.BlockSpec((None, n1f * DFT_N, LANES), lambda c, p: (order, 0, c), pipeline_mode=once),
            pl.BlockSpec((None, n1f * DFT_N, LANES), lambda c, p: (order, 0, c), pipeline_mode=once),
            pl.BlockSpec((n1f * DFT_N, LANES), lambda c, p: (0, 0), pipeline_mode=once),
            pl.BlockSpec((n1f * DFT_N, LANES), lambda c, p: (0, 0), pipeline_mode=once),
            pl.BlockSpec((2 * DFT_N, DFT_N), lambda c, p: (0, 0)),
            pl.BlockSpec((1, LANES), lambda c, p: (0, c)),
        ],
        out_specs=pl.BlockSpec((2 * seq, LANES), lambda c, p: (p, c)),
        out_shape=jax.ShapeDtypeStruct((nb * seq, D_GROUP), out_dtype),
        scratch_shapes=[pltpu.VMEM((n1f * DFT_N, 2 * LANES), F32)],
        compiler_params=_params(("arbitrary", "arbitrary")),
        name="hyena_long_conv",
    )(u, g, spec_r, spec_i, tw[0], tw[1], fst, bias[order][None, :])


def _hidden_body(f_ref, w1_ref, b1_ref, w2_ref, b2_ref, fr_ref, o_ref):
    hi = lax.Precision.HIGHEST
    fr = fr_ref[...]
    h = jnp.sin(fr * (jnp.dot(f_ref[...], w1_ref[...], precision=hi, preferred_element_type=F32) + b1_ref[...]))
    o_ref[...] = jnp.sin(fr * (jnp.dot(h, w2_ref[...], precision=hi, preferred_element_type=F32) + b2_ref[...]))


def _filter_features(seq):
    bands = np.linspace(1e-4, HY_BANDS - 1, HY_BANDS)[None, :]

    def feats(j):
        wpos = (2.0 * math.pi / seq) * j
        return np.concatenate([j / (seq - 1.0), np.cos(bands * wpos), -np.sin(bands * wpos)], axis=-1)

    j = np.arange(seq, dtype=np.float64)[:, None]
    return jnp.asarray(np.concatenate([feats(j), feats(seq - j)], axis=1), F32)


def _blockdiag2(w):
    z = jnp.zeros_like(w)
    return jnp.concatenate([jnp.concatenate([w, z], axis=1), jnp.concatenate([z, w], axis=1)], axis=0)


def _filter_hidden(feats, w1, b1, w2, b2, freq):
    seq = feats.shape[0]
    two = lambda v: jnp.concatenate([v, v])[None, :]
    tr = 512
    full = lambda a: pl.BlockSpec(a.shape, lambda i: (0, 0))
    args = (_blockdiag2(w1), two(b1), _blockdiag2(w2), two(b2), two(freq))
    return pl.pallas_call(
        _hidden_body,
        grid=(seq // tr,),
        in_specs=[pl.BlockSpec((tr, 2 * HY_EMB), lambda i: (i, 0))] + [full(a) for a in args],
        out_specs=pl.BlockSpec((tr, 2 * HY_HIDDEN), lambda i: (i, 0)),
        out_shape=jax.ShapeDtypeStruct((seq, 2 * HY_HIDDEN), F32),
        compiler_params=_params(("parallel",)),
        name="hyena_filter_hidden",
    )(feats, *args)


FILTER_ROWS = 512


def _hfilter_body(h_ref, w3_ref, d_ref, twr_ref, twi_ref, fst_ref, or_ref, oi_ref, k_scr, q_scr, *, seq):
    n1f = 2 * seq // DFT_N
    dec = jnp.abs(d_ref[...])
    w3 = w3_ref[...]
    norm = jnp.zeros((1, LANES), F32)
    for r0 in range(0, seq, FILTER_ROWS):
        j = (lax.broadcasted_iota(jnp.int32, (FILTER_ROWS, 1), 0) + r0).astype(F32)
        k2 = jnp.dot(h_ref[r0:r0 + FILTER_ROWS, :], w3, precision=lax.Precision.HIGHEST, preferred_element_type=F32)
        kf = k2[:, :LANES] * jnp.exp(-(j / (seq - 1.0)) * dec[:, :LANES])
        kb = k2[:, LANES:] * jnp.exp(-((seq - j) / (seq - 1.0)) * dec[:, LANES:])
        kb = jnp.where(j > 0.0, kb, 0.0)
        norm = norm + jnp.sum(jnp.abs(kf), axis=0, keepdims=True) + jnp.sum(jnp.abs(kb), axis=0, keepdims=True)
        k_scr[r0:r0 + FILTER_ROWS, :] = kf
        k_scr[seq + r0:seq + r0 + FILTER_ROWS, :] = kb
    scale = (1.0 / (2 * seq)) / norm
    for r0 in range(0, 2 * seq, FILTER_ROWS):
        k_scr[r0:r0 + FILTER_ROWS, :] = k_scr[r0:r0 + FILTER_ROWS, :] * scale
    fst = fst_ref[...]

    def fwd(i, carry):
        r = pl.multiple_of(i * SUBLANES, SUBLANES)
        xs = [(k_scr[pl.ds(t1 * DFT_N + r, SUBLANES), :], None) for t1 in range(n1f)]
        _store_rows(q_scr, _forward_rows(xs, n1f), r, xs[0][0])
        return carry

    lax.fori_loop(0, DFT_N // SUBLANES, fwd, 0)

    def mid(i, carry):
        rows = [_block_rows(i * MID_BLOCKS + k) for k in range(MID_BLOCKS)]
        x = [_twiddled_block_dft(fst, q_scr, twr_ref, twi_ref, r) for r in rows]
        for r, (xr, xi) in zip(rows, x):
            or_ref[r, :] = xr
            oi_ref[r, :] = xi
        return carry

    lax.fori_loop(0, n1f // MID_BLOCKS, mid, 0)


def _by_block(a):
    nc = D_GROUP // LANES
    lead = a.shape[:-1]
    a = a.reshape(lead + (HY_ORDER, 2, nc, LANES))
    return jnp.swapaxes(a, -3, -2).reshape(lead + (HY_ORDER * nc * 2 * LANES,))


def _hfilter(hid, w3, decay, tw, fst, seq):
    n1f = 2 * seq // DFT_N
    nc = D_GROUP // LANES
    direction = jnp.arange(w3.shape[1]) // D_GROUP % 2
    w3 = jnp.concatenate([jnp.where(direction == 0, w3, 0.0), jnp.where(direction == 1, w3, 0.0)], axis=0)
    w3 = _by_block(w3)
    dec = _by_block(decay)[None, :]
    once = pl.Buffered(1)
    pair = lambda o, c: (0, nc * o + c)
    out_spec = pl.BlockSpec((None, n1f * DFT_N, LANES), lambda o, c: (o, 0, c))
    out_shape = jax.ShapeDtypeStruct((HY_ORDER, n1f * DFT_N, D_GROUP), F32)
    return pl.pallas_call(
        functools.partial(_hfilter_body, seq=seq),
        grid=(HY_ORDER, nc),
        in_specs=[
            pl.BlockSpec((seq, 2 * HY_HIDDEN), lambda o, c: (0, 0)),
            pl.BlockSpec((2 * HY_HIDDEN, 2 * LANES), pair),
            pl.BlockSpec((1, 2 * LANES), pair),
            pl.BlockSpec((n1f * DFT_N, LANES), lambda o, c: (0, 0), pipeline_mode=once),
            pl.BlockSpec((n1f * DFT_N, LANES), lambda o, c: (0, 0), pipeline_mode=once),
            pl.BlockSpec((2 * DFT_N, DFT_N), lambda o, c: (0, 0)),
        ],
        out_specs=[out_spec, out_spec],
        out_shape=[out_shape, out_shape],
        scratch_shapes=[pltpu.VMEM((2 * seq, LANES), F32), pltpu.VMEM((n1f * DFT_N, 2 * LANES), F32)],
        compiler_params=_params(("arbitrary", "arbitrary")),
        name="hyena_filter_spectrum",
    )(hid, w3, dec, tw[0], tw[1], fst)


def _trunk(x, p, fst):
    nb, seq, _ = x.shape
    x2d = x.reshape(nb * seq, D_MODEL)
    tw = _twiddle_table(2 * seq // DFT_N)
    feats = _filter_features(seq)
    vcol = 0
    g1col = D_GROUP // LANES
    g2col = 2 * D_GROUP // LANES
    for l in range(DEPTH):
        z = _inproj(x2d, p["mix_norm_g"][l][None, :], p["w_in"], l)
        out_a = _pool(z, p["pool_w"], p["pool_scale"][l][None, :], l, nb, seq)
        out_b = _attention(z, p["attn_sink"][l], nb, seq)
        out_c = _sconv(z, p["sconv_w"][l], nb, seq)
        hz = _hshort(z, p["hy_short_w"][l], nb, seq)
        hid = _filter_hidden(feats, p["hy_w1"][l], p["hy_b1"][l], p["hy_w2"][l], p["hy_b2"][l], p["hy_freq"][l])
        spec_r, spec_i = _hfilter(hid, p["hy_w3"][l], p["hy_decay"][l], tw, fst, seq)
        bias = p["hy_bias"][l]
        u1 = _hconv(hz, vcol, hz, g1col, spec_r, spec_i, 0, tw, fst, bias, nb, seq, F32)
        out_d = _hconv(u1, 0, hz, g2col, spec_r, spec_i, 1, tw, fst, bias, nb, seq, BF16)
        x2d = _outproj(x2d, (out_a, out_b, out_c, out_d), p["w_out"], l)
        x2d = _ffn(x2d, p["ffn_norm_g"][l][None, :], p["w_gate_up"], p["w_down"],
                   p["final_norm_g"][None, :], l, final_norm=(l == DEPTH - 1))
    return x2d.reshape(nb, seq, D_MODEL)


def kernel(x_prompt, x_sample, mix_norm_g, w_in, pool_w, pool_scale, attn_sink, sconv_w, hy_short_w, hy_w1, hy_b1, hy_w2, hy_b2, hy_w3, hy_freq, hy_decay, hy_bias, w_out, ffn_norm_g, w_gate_up, w_down, final_norm_g):
    p = dict(
        mix_norm_g=mix_norm_g, w_in=w_in.astype(BF16), pool_w=pool_w.astype(BF16), pool_scale=pool_scale,
        attn_sink=attn_sink, sconv_w=sconv_w, hy_short_w=hy_short_w, hy_w1=hy_w1, hy_b1=hy_b1, hy_w2=hy_w2,
        hy_b2=hy_b2, hy_w3=hy_w3, hy_freq=hy_freq, hy_decay=hy_decay, hy_bias=hy_bias,
        w_out=w_out.astype(BF16), ffn_norm_g=ffn_norm_g, w_gate_up=w_gate_up.astype(BF16),
        w_down=w_down.astype(BF16), final_norm_g=final_norm_g,
    )
    fst = _dft_stack().astype(BF16)
    return (_trunk(x_prompt, p, fst), _trunk(x_sample, p, fst))
```

```python
import functools
import math

import numpy as np
import jax
import jax.numpy as jnp
from jax import lax
from jax.experimental import pallas as pl
from jax.experimental.pallas import tpu as pltpu

F32 = jnp.float32
BF16 = jnp.bfloat16

D_MODEL = 2048
DEPTH = 2
D_GROUP = 512
POOL_WINDOWS = (2, 4, 8, 16)
POOL_CH = 128
HEAD_DIM = 64
N_Q_HEADS = 8
N_KV_HEADS = 2
GQA_GROUP = 4
BLOCK = 128
ROPE_DIM = 16
ROPE_THETA = 500000.0
HY_ORDER = 2
HY_EMB = 33
HY_BANDS = 16
HY_HIDDEN = 64
D_FF = 5632
NORM_EPS = 1e-6
OFF_Q = 512
OFF_K = 1024
OFF_V = 1152
OFF_C = 1280
OFF_D = 2816
D_IN = 4352

LANES = 128
SUBLANES = 8
DFT_N = 256
VMEM_LIMIT = 56 * 2 ** 20


def _params(sem, vmem=VMEM_LIMIT):
    return pltpu.CompilerParams(dimension_semantics=sem, vmem_limit_bytes=vmem)


def _rms(x, g):
    ms = jnp.mean(x * x, axis=-1, keepdims=True)
    return x * lax.rsqrt(ms + NORM_EPS) * g


def _inproj_body(x_ref, g_ref, w_ref, o_ref):
    hn = _rms(x_ref[...], g_ref[...]).astype(BF16)
    o_ref[...] = jnp.dot(hn, w_ref[...], preferred_element_type=F32)


def _inproj(x2d, g, w, layer):
    T = x2d.shape[0]
    tm = 512
    return pl.pallas_call(
        _inproj_body,
        grid=(T // tm,),
        in_specs=[
            pl.BlockSpec((tm, D_MODEL), lambda i: (i, 0)),
            pl.BlockSpec((1, D_MODEL), lambda i: (0, 0)),
            pl.BlockSpec((None, D_MODEL, D_IN), lambda i: (layer, 0, 0), pipeline_mode=pl.Buffered(1)),
        ],
        out_specs=pl.BlockSpec((tm, D_IN), lambda i: (i, 0)),
        out_shape=jax.ShapeDtypeStruct((T, D_IN), F32),
        compiler_params=_params(("parallel",)),
        name="inproj",
    )(x2d, g, w)


def _outproj_body(x_ref, a_ref, b_ref, c_ref, d_ref, w_ref, o_ref):
    acc = x_ref[...]
    for k, m_ref in enumerate((a_ref, b_ref, c_ref, d_ref)):
        acc = acc + jnp.dot(m_ref[...], w_ref[k * D_GROUP:(k + 1) * D_GROUP, :], preferred_element_type=F32)
    o_ref[...] = acc


def _outproj(x2d, mixes, w, layer):
    T = x2d.shape[0]
    tm = 512
    mix_spec = pl.BlockSpec((tm, D_GROUP), lambda i: (i, 0))
    return pl.pallas_call(
        _outproj_body,
        grid=(T // tm,),
        in_specs=[pl.BlockSpec((tm, D_MODEL), lambda i: (i, 0)), mix_spec, mix_spec, mix_spec, mix_spec,
                  pl.BlockSpec((None, D_MODEL, D_MODEL), lambda i: (layer, 0, 0), pipeline_mode=pl.Buffered(1))],
        out_specs=pl.BlockSpec((tm, D_MODEL), lambda i: (i, 0)),
        out_shape=jax.ShapeDtypeStruct((T, D_MODEL), F32),
        compiler_params=_params(("parallel",)),
        name="outproj",
    )(x2d, *mixes, w)


def _ffn_body(x_ref, g_ref, wg_ref, wu_ref, wd_ref, gf_ref, o_ref, h_ref, *, final_norm):
    j = pl.program_id(1)

    @pl.when(j == 0)
    def _():
        x = x_ref[...]
        h_ref[...] = _rms(x, g_ref[...]).astype(BF16)
        o_ref[...] = x

    h = h_ref[...]
    gate = jnp.dot(h, wg_ref[...], preferred_element_type=F32)
    up = jnp.dot(h, wu_ref[...], preferred_element_type=F32)
    act = (gate * jax.nn.sigmoid(gate) * up).astype(BF16)
    o_ref[...] += jnp.dot(act, wd_ref[...], preferred_element_type=F32)

    if final_norm:
        @pl.when(j == pl.num_programs(1) - 1)
        def _():
            o_ref[...] = _rms(o_ref[...], gf_ref[...])


def _ffn(x2d, g, w_gate_up, w_down, gf, layer, final_norm):
    T = x2d.shape[0]
    tm, tf = 1024, 512
    nf = D_FF // tf
    return pl.pallas_call(
        functools.partial(_ffn_body, final_norm=final_norm),
        grid=(T // tm, nf),
        in_specs=[
            pl.BlockSpec((tm, D_MODEL), lambda i, j: (i, 0)),
            pl.BlockSpec((1, D_MODEL), lambda i, j: (0, 0)),
            pl.BlockSpec((None, D_MODEL, tf), lambda i, j: (layer, 0, j)),
            pl.BlockSpec((None, D_MODEL, tf), lambda i, j: (layer, 0, j + nf)),
            pl.BlockSpec((None, tf, D_MODEL), lambda i, j: (layer, j, 0)),
            pl.BlockSpec((1, D_MODEL), lambda i, j: (0, 0)),
        ],
        out_specs=pl.BlockSpec((tm, D_MODEL), lambda i, j: (i, 0)),
        out_shape=jax.ShapeDtypeStruct((T, D_MODEL), F32),
        scratch_shapes=[pltpu.VMEM((tm, D_MODEL), BF16)],
        compiler_params=_params(("parallel", "arbitrary")),
        name="ffn",
    )(x2d, g, w_gate_up, w_gate_up, w_down, gf)


def _shift_rows(x, s):
    n = x.shape[0]
    r = pltpu.roll(x, s % n, axis=0)
    t = lax.broadcasted_iota(jnp.int32, x.shape, 0)
    keep = (t >= s) if s > 0 else (t < n + s)
    return jnp.where(keep, r, 0.0)


def _conv3(x, w_ref):
    return _shift_rows(x, 1) * w_ref[0:1, :] + x * w_ref[1:2, :] + _shift_rows(x, -1) * w_ref[2:3, :]


POOL_PAD = 16


def _pool_body(p_ref, w_ref, s_ref, o_ref, pad_ref, *, seq):
    c = pl.program_id(1)
    p = p_ref[...]
    zeros = jnp.zeros((POOL_PAD, LANES), F32)
    pad_ref[0:POOL_PAD, :] = zeros
    pad_ref[POOL_PAD + seq:2 * POOL_PAD + seq, :] = zeros
    pad_ref[POOL_PAD:POOL_PAD + seq, :] = p
    n = seq + 2 * POOL_PAD
    t = lax.broadcasted_iota(jnp.int32, (seq, 1), 0)

    for k, win in enumerate(POOL_WINDOWS):
        @pl.when(c == k)
        def _(win=win):
            s = pad_ref[...]
            s = s + pltpu.roll(s, 1, axis=0)
            step = 1
            while 2 * step < win:
                s = pltpu.roll(s, step, axis=0) + pltpu.roll(s, n - step, axis=0)
                step *= 2
            s = s[POOL_PAD:POOL_PAD + seq, :]
            half = win // 2
            cnt = (jnp.minimum(t + half, seq) - jnp.maximum(t - half, 0)).astype(F32)
            d = s / cnt - p
            out = jnp.dot(d.astype(BF16), w_ref[...], preferred_element_type=F32)
            o_ref[...] = (out * s_ref[...]).astype(o_ref.dtype)


def _pool(z2d, pool_w, pool_scale, layer, nb, seq):
    return pl.pallas_call(
        functools.partial(_pool_body, seq=seq),
        grid=(nb, len(POOL_WINDOWS)),
        in_specs=[
            pl.BlockSpec((seq, POOL_CH), lambda b, c: (b, c)),
            pl.BlockSpec((None, None, POOL_CH, POOL_CH), lambda b, c: (layer, c, 0, 0)),
            pl.BlockSpec((1, POOL_CH), lambda b, c: (0, c)),
        ],
        out_specs=pl.BlockSpec((seq, POOL_CH), lambda b, c: (b, c)),
        out_shape=jax.ShapeDtypeStruct((nb * seq, D_GROUP), BF16),
        scratch_shapes=[pltpu.VMEM((seq + 2 * POOL_PAD, LANES), F32)],
        compiler_params=_params(("parallel", "parallel")),
        name="pool_mixer",
    )(z2d, pool_w, pool_scale)


def _sconv_body(h_ref, b_ref, c_ref, w_ref, o_ref):
    o_ref[...] = (b_ref[...] * _conv3(c_ref[...] * h_ref[...], w_ref)).astype(o_ref.dtype)


def _local_width(seq):
    return 2 * LANES if seq <= 2048 else LANES


def _sconv(z2d, w, nb, seq):
    wid = _local_width(seq)
    nc = D_GROUP // wid
    c0 = OFF_C // wid
    return pl.pallas_call(
        _sconv_body,
        grid=(nb, nc),
        in_specs=[
            pl.BlockSpec((seq, wid), lambda b, c: (b, c0 + c)),
            pl.BlockSpec((seq, wid), lambda b, c: (b, c0 + nc + c)),
            pl.BlockSpec((seq, wid), lambda b, c: (b, c0 + 2 * nc + c)),
            pl.BlockSpec((3, wid), lambda b, c: (0, c)),
        ],
        out_specs=pl.BlockSpec((seq, wid), lambda b, c: (b, c)),
        out_shape=jax.ShapeDtypeStruct((nb * seq, D_GROUP), BF16),
        compiler_params=_params(("parallel", "parallel")),
        name="gated_sconv",
    )(z2d, z2d, z2d, w)


def _hshort_body(z_ref, w_ref, o_ref):
    o_ref[...] = _conv3(z_ref[...], w_ref)


def _hshort(z2d, w, nb, seq):
    wid = _local_width(seq)
    nc = 3 * D_GROUP // wid
    c0 = OFF_D // wid
    return pl.pallas_call(
        _hshort_body,
        grid=(nb, nc),
        in_specs=[pl.BlockSpec((seq, wid), lambda b, c: (b, c0 + c)),
                  pl.BlockSpec((3, wid), lambda b, c: (0, c))],
        out_specs=pl.BlockSpec((seq, wid), lambda b, c: (b, c)),
        out_shape=jax.ShapeDtypeStruct((nb * seq, 3 * D_GROUP), F32),
        compiler_params=_params(("parallel", "parallel")),
        name="hyena_short_conv",
    )(z2d, w)


def _rope_tables(seq):
    inv_freq = ROPE_THETA ** (-np.arange(0, ROPE_DIM, 2, dtype=np.float64) / ROPE_DIM)
    ang = np.arange(seq, dtype=np.float64)[:, None] * inv_freq[None, :]
    half = ROPE_DIM // 2
    cos = np.ones((seq, HEAD_DIM))
    s_lo = np.zeros((seq, HEAD_DIM))
    s_hi = np.zeros((seq, HEAD_DIM))
    cos[:, :half] = np.cos(ang)
    cos[:, half:ROPE_DIM] = np.cos(ang)
    s_lo[:, half:ROPE_DIM] = np.sin(ang)
    s_hi[:, :half] = -np.sin(ang)
    rep = LANES // HEAD_DIM
    return tuple(jnp.asarray(np.tile(a, (1, rep)), F32) for a in (cos, s_lo, s_hi))


LOG2E = math.log2(math.e)
ATTN_QBLOCKS = 4


def _rope(x, cos, s_lo, s_hi):
    half = ROPE_DIM // 2
    return x * cos + pltpu.roll(x, half, axis=1) * s_lo + pltpu.roll(x, LANES - half, axis=1) * s_hi


def _attn_body(q_ref, kp_ref, kc_ref, kn_ref, vp_ref, vc_ref, vn_ref, cq_ref, cp_ref, cn_ref, sink_ref, o_ref):
    n = pl.program_id(1)
    nstep = pl.num_programs(1)
    two = 2 * BLOCK
    blk_rows = lambda i: slice(i * BLOCK, (i + 1) * BLOCK)
    tq = tuple(r[...] for r in cq_ref)

    q = q_ref[...]
    qscale = HEAD_DIM ** -0.5 * LOG2E
    qg = [(_rope(q[:, g * LANES:(g + 1) * LANES], *tq) * qscale).astype(BF16) for g in range(D_GROUP // LANES)]

    lane = lax.broadcasted_iota(jnp.int32, (BLOCK, LANES), 1)
    first = lane < HEAD_DIM
    half_ones = jnp.where(first, 1.0, 0.0)
    ones_bd = jnp.concatenate([half_ones, 1.0 - half_ones], axis=0)

    def per_head(x):
        xr = pltpu.roll(x, HEAD_DIM, axis=1)
        return [jnp.concatenate([jnp.where(first, x, 0.0), jnp.where(first, 0.0, xr)], axis=0),
                jnp.concatenate([jnp.where(first, xr, 0.0), jnp.where(first, 0.0, x)], axis=0)]

    kc = _rope(kc_ref[...], *tq)
    vc = vc_ref[...]
    kblocks = ([_rope(kp_ref[...], *(r[...] for r in cp_ref))] + [kc[blk_rows(i)] for i in range(ATTN_QBLOCKS)]
               + [_rope(kn_ref[...], *(r[...] for r in cn_ref))])
    vblocks = [vp_ref[...]] + [vc[blk_rows(i)] for i in range(ATTN_QBLOCKS)] + [vn_ref[...]]
    kcat = [[a.astype(BF16) for a in per_head(k)] for k in kblocks]
    vcat = [[jnp.concatenate([a, ones_bd], axis=1).astype(BF16) for a in per_head(v)] for v in vblocks]

    r = lax.broadcasted_iota(jnp.int32, (two, two), 0) % BLOCK
    c = lax.broadcasted_iota(jnp.int32, (two, two), 1)
    left = c < BLOCK
    c = c % BLOCK
    below, above = c >= r, c <= r
    first2 = lax.broadcasted_iota(jnp.int32, (two, LANES), 1) < HEAD_DIM

    for i in range(ATTN_QBLOCKS):
        lo_mask = (below & (n > 0)) if i == 0 else below
        hi_mask = (above & (n < nstep - 1)) if i == ATTN_QBLOCKS - 1 else above
        masks = [lo_mask, None, hi_mask]
        outs = []
        for kv in range(N_KV_HEADS):
            qs = jnp.concatenate([qg[2 * kv][blk_rows(i)], qg[2 * kv + 1][blk_rows(i)]], axis=0)
            s = []
            for j in range(3):
                sb = lax.dot_general(qs, kcat[i + j][kv], (((1,), (1,)), ((), ())), preferred_element_type=F32)
                if masks[j] is not None:
                    sb = jnp.where(masks[j], sb, -jnp.inf)
                s.append(sb)
            sink = sink_ref[kv]
            top = jnp.maximum(jnp.maximum(s[0], s[1]), s[2])
            m = jnp.where(left, jnp.max(top[:, :BLOCK], axis=-1, keepdims=True),
                          jnp.max(top[:, BLOCK:], axis=-1, keepdims=True))
            m = jnp.maximum(m, sink)
            acc = sum(jnp.dot(jnp.exp2(s[j] - m).astype(BF16), vcat[i + j][kv], preferred_element_type=F32)
                      for j in range(3))
            m_out = jnp.where(first2, m[:, :LANES], m[:, LANES:])
            sink_out = jnp.where(first2, sink[:, :LANES], sink[:, LANES:])
            o = acc[:, :LANES] / (acc[:, LANES:] + jnp.exp2(sink_out - m_out))
            outs += [o[:BLOCK], o[BLOCK:]]
        o_ref[blk_rows(i), :] = jnp.concatenate(outs, axis=1).astype(o_ref.dtype)


def _attention(z2d, sink, nb, seq):
    nblk = seq // BLOCK
    nstep = nblk // ATTN_QBLOCKS
    rows = ATTN_QBLOCKS * BLOCK
    tabs = _rope_tables(seq)
    sink_tab = jnp.repeat(jnp.repeat((sink.astype(F32) * LOG2E).reshape(N_KV_HEADS, 2, 2), BLOCK, axis=1),
                          BLOCK, axis=2)
    kcol, vcol = OFF_K // LANES, OFF_V // LANES

    def prev(n):
        return jnp.maximum(n * ATTN_QBLOCKS - 1, 0)

    def nxt(n):
        return jnp.minimum((n + 1) * ATTN_QBLOCKS, nblk - 1)

    def own(width, col):
        return pl.BlockSpec((rows, width), lambda b, n: (b * nstep + n, col))

    def edge(col, sel):
        return pl.BlockSpec((BLOCK, LANES), lambda b, n: (b * nblk + sel(n), col))

    def tab_edge(sel):
        return [pl.BlockSpec((BLOCK, LANES), lambda b, n: (sel(n), 0))] * 3

    return pl.pallas_call(
        _attn_body,
        grid=(nb, nstep),
        in_specs=[
            own(D_GROUP, OFF_Q // D_GROUP),
            edge(kcol, prev), own(LANES, kcol), edge(kcol, nxt),
            edge(vcol, prev), own(LANES, vcol), edge(vcol, nxt),
            [pl.BlockSpec((rows, LANES), lambda b, n: (n, 0))] * 3, tab_edge(prev), tab_edge(nxt),
            pl.BlockSpec((N_KV_HEADS, 2 * BLOCK, 2 * BLOCK), lambda b, n: (0, 0, 0)),
        ],
        out_specs=pl.BlockSpec((rows, D_GROUP), lambda b, n: (b * nstep + n, 0)),
        out_shape=jax.ShapeDtypeStruct((nb * seq, D_GROUP), BF16),
        compiler_params=_params(("parallel", "parallel")),
        name="window_attention",
    )(z2d, z2d, z2d, z2d, z2d, z2d, z2d, list(tabs), list(tabs), list(tabs), sink_tab)


def _is(x, v):
    return abs(x - v) < 1e-12


def _scale(a, s):
    if a is None or _is(s, 0.0):
        return None
    if _is(s, 1.0):
        return a
    if _is(s, -1.0):
        return -a
    return a * s


def _cmul_const(a, w):
    ar, ai = a
    return (_psub(_scale(ar, w.real), _scale(ai, w.imag)), _padd(_scale(ar, w.imag), _scale(ai, w.real)))


def _padd(x, y):
    if x is None:
        return y
    if y is None:
        return x
    return x + y


def _psub(x, y):
    if y is None:
        return x
    if x is None:
        return -y
    return x - y


def _fft_blocks(xs, sign):
    n = len(xs)
    if n == 1:
        return list(xs)
    ev = _fft_blocks(xs[0::2], sign)
    od = _fft_blocks(xs[1::2], sign)
    out = [None] * n
    for k in range(n // 2):
        t = _cmul_const(od[k], np.exp(sign * 2j * math.pi * k / n))
        out[k] = (_padd(ev[k][0], t[0]), _padd(ev[k][1], t[1]))
        out[k + n // 2] = (_psub(ev[k][0], t[0]), _psub(ev[k][1], t[1]))
    return out


def _dense(a, like):
    return jnp.zeros_like(like) if a is None else a


def _dft_stack():
    k = np.arange(DFT_N, dtype=np.float64)
    ang = 2.0 * math.pi * np.outer(k, k) / DFT_N
    return jnp.asarray(np.concatenate([np.cos(ang), -np.sin(ang)], axis=0), F32)


def _twiddle_table(n1f):
    f1 = np.arange(n1f, dtype=np.float64)[:, None]
    t2 = np.arange(DFT_N, dtype=np.float64)[None, :]
    ang = (-2.0 * math.pi / (n1f * DFT_N)) * f1 * t2
    rep = lambda a: jnp.asarray(np.repeat(a.reshape(-1, 1), LANES, axis=1), F32)
    return rep(np.cos(ang)), rep(np.sin(ang))


def _block_dft(fst, qr, qi):
    rhs = jnp.concatenate([qr, qi], axis=1).astype(BF16)
    d = jnp.dot(fst, rhs, preferred_element_type=F32)
    return d[0:DFT_N, 0:LANES], d[0:DFT_N, LANES:], d[DFT_N:, 0:LANES], d[DFT_N:, LANES:]


def _forward_rows(xs, n1f):
    if len(xs) == n1f:
        return _fft_blocks(xs, -1.0)
    ev = _fft_blocks(xs, -1.0)
    od = _fft_blocks([_cmul_const(x, np.exp(-2j * math.pi * t / n1f)) for t, x in enumerate(xs)], -1.0)
    out = [None] * n1f
    out[0::2] = ev
    out[1::2] = od
    return out


MID_BLOCKS = 4


def _block_rows(f):
    return pl.ds(pl.multiple_of(f * DFT_N, DFT_N), DFT_N)


def _store_rows(q_scr, blocks, r, like):
    for f, (pr, pi) in enumerate(blocks):
        rows = pl.ds(f * DFT_N + r, SUBLANES)
        q_scr[rows, 0:LANES] = _dense(pr, like)
        q_scr[rows, LANES:] = _dense(pi, like)


def _twiddled_block_dft(fst, q_scr, twr_ref, twi_ref, rows):
    pr, pi = q_scr[rows, 0:LANES], q_scr[rows, LANES:]
    tr, ti = twr_ref[rows, :], twi_ref[rows, :]
    cr, ci, sr, si = _block_dft(fst, pr * tr - pi * ti, pr * ti + pi * tr)
    return cr - si, ci + sr


def _hconv_body(u_ref, g_ref, kr_ref, ki_ref, twr_ref, twi_ref, fst_ref, b_ref, o_ref, q_scr, *, seq):
    n1 = seq // DFT_N
    n1f = 2 * n1
    fst = fst_ref[...]

    def fwd(i, carry):
        r = pl.multiple_of(i * SUBLANES, SUBLANES)
        xs = [(u_ref[pl.ds(t * DFT_N + r, SUBLANES), :], u_ref[pl.ds(seq + t * DFT_N + r, SUBLANES), :])
              for t in range(n1)]
        _store_rows(q_scr, _forward_rows(xs, n1f), r, xs[0][0])
        return carry

    lax.fori_loop(0, DFT_N // SUBLANES, fwd, 0)

    def mid(i, carry):
        rows = [_block_rows(i * MID_BLOCKS + k) for k in range(MID_BLOCKS)]
        x = [_twiddled_block_dft(fst, q_scr, twr_ref, twi_ref, r) for r in rows]
        y = []
        for r, (xr, xi) in zip(rows, x):
            kr, ki = kr_ref[r, :].astype(F32), ki_ref[r, :].astype(F32)
            y.append((xr * kr - xi * ki, xr * ki + xi * kr))
        e = [_block_dft(fst, yr, yi) for yr, yi in y]
        s = []
        for r, (cr, ci, sr, si) in zip(rows, e):
            rr, ri = cr + si, ci - sr
            tr, ti = twr_ref[r, :], twi_ref[r, :]
            s.append((rr * tr + ri * ti, ri * tr - rr * ti))
        for r, (sr_, si_) in zip(rows, s):
            q_scr[r, 0:LANES] = sr_
            q_scr[r, LANES:] = si_
        return carry

    lax.fori_loop(0, n1f // MID_BLOCKS, mid, 0)

    bias = b_ref[...]

    def inv(i, carry):
        r = pl.multiple_of(i * SUBLANES, SUBLANES)
        ss = [(q_scr[pl.ds(f * DFT_N + r, SUBLANES), 0:LANES], q_scr[pl.ds(f * DFT_N + r, SUBLANES), LANES:])
              for f in range(n1f)]
        ev = _fft_blocks(ss[0::2], 1.0)
        od = _fft_blocks(ss[1::2], 1.0)
        for t in range(n1):
            o = _cmul_const(od[t], np.exp(2j * math.pi * t / n1f))
            for b, y in enumerate((ev[t][0] + o[0], ev[t][1] + o[1])):
                rows = pl.ds(b * seq + t * DFT_N + r, SUBLANES)
                o_ref[rows, :] = (g_ref[rows, :] * (y + bias * u_ref[rows, :])).astype(o_ref.dtype)
        return carry

    lax.fori_loop(0, DFT_N // SUBLANES, inv, 0)


def _hconv(u, ucol, g, gcol, spec_r, spec_i, order, tw, fst, bias, nb, seq, out_dtype):
    n1f = 2 * seq // DFT_N
    nc = D_GROUP // LANES
    once = pl.Buffered(1)
    return pl.pallas_call(
        functools.partial(_hconv_body, seq=seq),
        grid=(nc, nb // 2),
        in_specs=[
            pl.BlockSpec((2 * seq, LANES), lambda c, p: (p, ucol + c)),
            pl.BlockSpec((2 * seq, LANES), lambda c, p: (p, gcol + c)),
            pl.BlockSpec((None, n1f * DFT_N, LANES), lambda c, p: (order, 0, c)),
            pl.BlockSpec((None, n1f * DFT_N, LANES), lambda c, p: (order, 0, c)),
            pl.BlockSpec((n1f * DFT_N, LANES), lambda c, p: (0, 0), pipeline_mode=once),
            pl.BlockSpec((n1f * DFT_N, LANES), lambda c, p: (0, 0), pipeline_mode=once),
            pl.BlockSpec((2 * DFT_N, DFT_N), lambda c, p: (0, 0)),
            pl.BlockSpec((1, LANES), lambda c, p: (0, c)),
        ],
        out_specs=pl.BlockSpec((2 * seq, LANES), lambda c, p: (p, c)),
        out_shape=jax.ShapeDtypeStruct((nb * seq, D_GROUP), out_dtype),
        scratch_shapes=[pltpu.VMEM((n1f * DFT_N, 2 * LANES), F32)],
        compiler_params=_params(("arbitrary", "arbitrary")),
        name="hyena_long_conv",
    )(u, g, spec_r, spec_i, tw[0], tw[1], fst, bias[order][None, :])


def _hidden_body(f_ref, w1_ref, b1_ref, w2_ref, b2_ref, fr_ref, o_ref):
    hi = lax.Precision.HIGHEST
    fr = fr_ref[...]
    h = jnp.sin(fr * (jnp.dot(f_ref[...], w1_ref[...], precision=hi, preferred_element_type=F32) + b1_ref[...]))
    o_ref[...] = jnp.sin(fr * (jnp.dot(h, w2_ref[...], precision=hi, preferred_element_type=F32) + b2_ref[...]))


def _filter_features(seq):
    bands = np.linspace(1e-4, HY_BANDS - 1, HY_BANDS)[None, :]

    def feats(j):
        wpos = (2.0 * math.pi / seq) * j
        return np.concatenate([j / (seq - 1.0), np.cos(bands * wpos), -np.sin(bands * wpos)], axis=-1)

    j = np.arange(seq, dtype=np.float64)[:, None]
    return jnp.asarray(np.concatenate([feats(j), feats(seq - j)], axis=1), F32)


def _blockdiag2(w):
    z = jnp.zeros_like(w)
    return jnp.concatenate([jnp.concatenate([w, z], axis=1), jnp.concatenate([z, w], axis=1)], axis=0)


def _filter_hidden(feats, w1, b1, w2, b2, freq):
    seq = feats.shape[0]
    two = lambda v: jnp.concatenate([v, v])[None, :]
    tr = 512
    full = lambda a: pl.BlockSpec(a.shape, lambda i: (0, 0))
    args = (_blockdiag2(w1), two(b1), _blockdiag2(w2), two(b2), two(freq))
    return pl.pallas_call(
        _hidden_body,
        grid=(seq // tr,),
        in_specs=[pl.BlockSpec((tr, 2 * HY_EMB), lambda i: (i, 0))] + [full(a) for a in args],
        out_specs=pl.BlockSpec((tr, 2 * HY_HIDDEN), lambda i: (i, 0)),
        out_shape=jax.ShapeDtypeStruct((seq, 2 * HY_HIDDEN), F32),
        compiler_params=_params(("parallel",)),
        name="hyena_filter_hidden",
    )(feats, *args)


FILTER_ROWS = 512


def _split_bf16(a):
    hi = a.astype(BF16)
    return hi, (a - hi.astype(F32)).astype(BF16)


def _hfilter_body(h_ref, w3_ref, d_ref, twr_ref, twi_ref, fst_ref, or_ref, oi_ref, k_scr, q_scr, *, seq):
    n1f = 2 * seq // DFT_N
    dec = jnp.abs(d_ref[...])
    w_hi, w_lo = _split_bf16(w3_ref[...])
    norm = jnp.zeros((1, LANES), F32)
    for r0 in range(0, seq, FILTER_ROWS):
        j = (lax.broadcasted_iota(jnp.int32, (FILTER_ROWS, 1), 0) + r0).astype(F32)
        h_hi, h_lo = _split_bf16(h_ref[r0:r0 + FILTER_ROWS, :])
        k2 = (jnp.dot(h_hi, w_hi, preferred_element_type=F32) + jnp.dot(h_hi, w_lo, preferred_element_type=F32)
              + jnp.dot(h_lo, w_hi, preferred_element_type=F32))
        kf = k2[:, :LANES] * jnp.exp(-(j / (seq - 1.0)) * dec[:, :LANES])
        kb = k2[:, LANES:] * jnp.exp(-((seq - j) / (seq - 1.0)) * dec[:, LANES:])
        kb = jnp.where(j > 0.0, kb, 0.0)
        norm = norm + jnp.sum(jnp.abs(kf), axis=0, keepdims=True) + jnp.sum(jnp.abs(kb), axis=0, keepdims=True)
        k_scr[r0:r0 + FILTER_ROWS, :] = kf
        k_scr[seq + r0:seq + r0 + FILTER_ROWS, :] = kb
    scale = (1.0 / (2 * seq)) / norm
    for r0 in range(0, 2 * seq, FILTER_ROWS):
        k_scr[r0:r0 + FILTER_ROWS, :] = k_scr[r0:r0 + FILTER_ROWS, :] * scale
    fst = fst_ref[...]

    def fwd(i, carry):
        r = pl.multiple_of(i * SUBLANES, SUBLANES)
        xs = [(k_scr[pl.ds(t1 * DFT_N + r, SUBLANES), :], None) for t1 in range(n1f)]
        _store_rows(q_scr, _forward_rows(xs, n1f), r, xs[0][0])
        return carry

    lax.fori_loop(0, DFT_N // SUBLANES, fwd, 0)

    def mid(i, carry):
        rows = [_block_rows(i * MID_BLOCKS + k) for k in range(MID_BLOCKS)]
        x = [_twiddled_block_dft(fst, q_scr, twr_ref, twi_ref, r) for r in rows]
        for r, (xr, xi) in zip(rows, x):
            or_ref[r, :] = xr.astype(or_ref.dtype)
            oi_ref[r, :] = xi.astype(oi_ref.dtype)
        return carry

    lax.fori_loop(0, n1f // MID_BLOCKS, mid, 0)


def _by_block(a):
    nc = D_GROUP // LANES
    lead = a.shape[:-1]
    a = a.reshape(lead + (HY_ORDER, 2, nc, LANES))
    return jnp.swapaxes(a, -3, -2).reshape(lead + (HY_ORDER * nc * 2 * LANES,))


def _hfilter(hid, w3, decay, tw, fst, seq):
    n1f = 2 * seq // DFT_N
    nc = D_GROUP // LANES
    direction = jnp.arange(w3.shape[1]) // D_GROUP % 2
    w3 = jnp.concatenate([jnp.where(direction == 0, w3, 0.0), jnp.where(direction == 1, w3, 0.0)], axis=0)
    w3 = _by_block(w3)
    dec = _by_block(decay)[None, :]
    once = pl.Buffered(1)
    pair = lambda o, c: (0, nc * o + c)
    out_spec = pl.BlockSpec((None, n1f * DFT_N, LANES), lambda o, c: (o, 0, c))
    out_shape = jax.ShapeDtypeStruct((HY_ORDER, n1f * DFT_N, D_GROUP), BF16)
    return pl.pallas_call(
        functools.partial(_hfilter_body, seq=seq),
        grid=(HY_ORDER, nc),
        in_specs=[
            pl.BlockSpec((seq, 2 * HY_HIDDEN), lambda o, c: (0, 0)),
            pl.BlockSpec((2 * HY_HIDDEN, 2 * LANES), pair),
            pl.BlockSpec((1, 2 * LANES), pair),
            pl.BlockSpec((n1f * DFT_N, LANES), lambda o, c: (0, 0), pipeline_mode=once),
            pl.BlockSpec((n1f * DFT_N, LANES), lambda o, c: (0, 0), pipeline_mode=once),
            pl.BlockSpec((2 * DFT_N, DFT_N), lambda o, c: (0, 0)),
        ],
        out_specs=[out_spec, out_spec],
        out_shape=[out_shape, out_shape],
        scratch_shapes=[pltpu.VMEM((2 * seq, LANES), F32), pltpu.VMEM((n1f * DFT_N, 2 * LANES), F32)],
        compiler_params=_params(("arbitrary", "arbitrary")),
        name="hyena_filter_spectrum",
    )(hid, w3, dec, tw[0], tw[1], fst)


def _trunk(x, p, fst):
    nb, seq, _ = x.shape
    x2d = x.reshape(nb * seq, D_MODEL)
    tw = _twiddle_table(2 * seq // DFT_N)
    feats = _filter_features(seq)
    vcol = 0
    g1col = D_GROUP // LANES
    g2col = 2 * D_GROUP // LANES
    for l in range(DEPTH):
        z = _inproj(x2d, p["mix_norm_g"][l][None, :], p["w_in"], l)
        out_a = _pool(z, p["pool_w"], p["pool_scale"][l][None, :], l, nb, seq)
        out_b = _attention(z, p["attn_sink"][l], nb, seq)
        out_c = _sconv(z, p["sconv_w"][l], nb, seq)
        hz = _hshort(z, p["hy_short_w"][l], nb, seq)
        hid = _filter_hidden(feats, p["hy_w1"][l], p["hy_b1"][l], p["hy_w2"][l], p["hy_b2"][l], p["hy_freq"][l])
        spec_r, spec_i = _hfilter(hid, p["hy_w3"][l], p["hy_decay"][l], tw, fst, seq)
        bias = p["hy_bias"][l]
        u1 = _hconv(hz, vcol, hz, g1col, spec_r, spec_i, 0, tw, fst, bias, nb, seq, F32)
        out_d = _hconv(u1, 0, hz, g2col, spec_r, spec_i, 1, tw, fst, bias, nb, seq, BF16)
        x2d = _outproj(x2d, (out_a, out_b, out_c, out_d), p["w_out"], l)
        x2d = _ffn(x2d, p["ffn_norm_g"][l][None, :], p["w_gate_up"], p["w_down"],
                   p["final_norm_g"][None, :], l, final_norm=(l == DEPTH - 1))
    return x2d.reshape(nb, seq, D_MODEL)


def kernel(x_prompt, x_sample, mix_norm_g, w_in, pool_w, pool_scale, attn_sink, sconv_w, hy_short_w, hy_w1, hy_b1, hy_w2, hy_b2, hy_w3, hy_freq, hy_decay, hy_bias, w_out, ffn_norm_g, w_gate_up, w_down, final_norm_g):
    p = dict(
        mix_norm_g=mix_norm_g, w_in=w_in.astype(BF16), pool_w=pool_w.astype(BF16), pool_scale=pool_scale,
        attn_sink=attn_sink, sconv_w=sconv_w, hy_short_w=hy_short_w, hy_w1=hy_w1, hy_b1=hy_b1, hy_w2=hy_w2,
        hy_b2=hy_b2, hy_w3=hy_w3, hy_freq=hy_freq, hy_decay=hy_decay, hy_bias=hy_bias,
        w_out=w_out.astype(BF16), ffn_norm_g=ffn_norm_g, w_gate_up=w_gate_up.astype(BF16),
        w_down=w_down.astype(BF16), final_norm_g=final_norm_g,
    )
    fst = _dft_stack().astype(BF16)
    return (_trunk(x_prompt, p, fst), _trunk(x_sample, p, fst))
```

```python
import functools
import math

import numpy as np
import jax
import jax.numpy as jnp
from jax import lax
from jax.experimental import pallas as pl
from jax.experimental.pallas import tpu as pltpu

F32 = jnp.float32
BF16 = jnp.bfloat16

D_MODEL = 2048
DEPTH = 2
D_GROUP = 512
POOL_WINDOWS = (2, 4, 8, 16)
POOL_CH = 128
HEAD_DIM = 64
N_Q_HEADS = 8
N_KV_HEADS = 2
GQA_GROUP = 4
BLOCK = 128
ROPE_DIM = 16
ROPE_THETA = 500000.0
HY_ORDER = 2
HY_EMB = 33
HY_BANDS = 16
HY_HIDDEN = 64
D_FF = 5632
NORM_EPS = 1e-6
OFF_Q = 512
OFF_K = 1024
OFF_V = 1152
OFF_C = 1280
OFF_D = 2816
D_IN = 4352

LANES = 128
SUBLANES = 8
DFT_N = 256
VMEM_LIMIT = 56 * 2 ** 20


def _params(sem, vmem=VMEM_LIMIT):
    return pltpu.CompilerParams(dimension_semantics=sem, vmem_limit_bytes=vmem)


def _rms(x, g):
    ms = jnp.mean(x * x, axis=-1, keepdims=True)
    return x * lax.rsqrt(ms + NORM_EPS) * g


HALO = SUBLANES
D_QKV = OFF_C - OFF_Q


def _conv3_ext(z, w_ref):
    n = z.shape[0]
    return (pltpu.roll(z, 1, axis=0) * w_ref[0:1, :] + z * w_ref[1:2, :]
            + pltpu.roll(z, n - 1, axis=0) * w_ref[2:3, :])


def _inmix_body(x_ref, xp_ref, xn_ref, g_ref, w_ref, pw_ref, ps_ref, cw_ref, hw_ref,
                qkv_ref, a_ref, c_ref, hz_ref, *, seq):
    tm = x_ref.shape[0]
    ext = tm + 2 * HALO
    core = slice(HALO, HALO + tm)
    g = g_ref[...]
    hn = jnp.concatenate([_rms(xp_ref[...], g), _rms(x_ref[...], g), _rms(xn_ref[...], g)], axis=0).astype(BF16)

    pos0 = (pl.program_id(0) * tm) % seq
    row = lax.broadcasted_iota(jnp.int32, (ext, 1), 0)
    inside = ((row >= HALO) | (pos0 > 0)) & ((row < HALO + tm) | (pos0 + tm < seq))

    def project(lo, hi):
        z = jnp.dot(hn, w_ref[:, lo:hi], preferred_element_type=F32)
        return jnp.where(inside, z, 0.0)

    qkv_ref[...] = project(OFF_Q, OFF_C)[core]

    za = project(0, OFF_Q)
    pos = pos0 + row - HALO
    for k, win in enumerate(POOL_WINDOWS):
        cols = slice(k * POOL_CH, (k + 1) * POOL_CH)
        p = za[:, cols]
        s = p + pltpu.roll(p, 1, axis=0)
        step = 1
        while 2 * step < win:
            s = pltpu.roll(s, step, axis=0) + pltpu.roll(s, ext - step, axis=0)
            step *= 2
        half = win // 2
        cnt = jnp.maximum(jnp.minimum(pos + half, seq) - jnp.maximum(pos - half, 0), 1).astype(F32)
        d = (s / cnt - p)[core]
        out = jnp.dot(d.astype(BF16), pw_ref[k], preferred_element_type=F32) * ps_ref[:, cols]
        a_ref[:, cols] = out.astype(a_ref.dtype)

    zc = project(OFF_C, OFF_D)
    ch, cb, cc = zc[:, :D_GROUP], zc[:, D_GROUP:2 * D_GROUP], zc[:, 2 * D_GROUP:]
    c_ref[...] = (cb * _conv3_ext(cc * ch, cw_ref))[core].astype(c_ref.dtype)

    hz_ref[...] = _conv3_ext(project(OFF_D, D_IN), hw_ref)[core]


def _inmix(x2d, g, w, pool_w, pool_scale, sconv_w, hy_short_w, layer, seq):
    T = x2d.shape[0]
    tm = 512
    per = tm // HALO
    last = T // HALO - 1
    full = lambda a: pl.BlockSpec(a.shape, lambda i: (0, 0))
    row_tile = lambda width: pl.BlockSpec((tm, width), lambda i: (i, 0))
    return pl.pallas_call(
        functools.partial(_inmix_body, seq=seq),
        grid=(T // tm,),
        in_specs=[
            row_tile(D_MODEL),
            pl.BlockSpec((HALO, D_MODEL), lambda i: (jnp.maximum(i * per - 1, 0), 0)),
            pl.BlockSpec((HALO, D_MODEL), lambda i: (jnp.minimum((i + 1) * per, last), 0)),
            full(g),
            pl.BlockSpec((None, D_MODEL, D_IN), lambda i: (layer, 0, 0), pipeline_mode=pl.Buffered(1)),
            pl.BlockSpec((None, len(POOL_WINDOWS), POOL_CH, POOL_CH), lambda i: (layer, 0, 0, 0)),
            full(pool_scale), full(sconv_w), full(hy_short_w),
        ],
        out_specs=[row_tile(D_QKV), row_tile(D_GROUP), row_tile(D_GROUP), row_tile(3 * D_GROUP)],
        out_shape=[jax.ShapeDtypeStruct((T, D_QKV), F32), jax.ShapeDtypeStruct((T, D_GROUP), BF16),
                   jax.ShapeDtypeStruct((T, D_GROUP), BF16), jax.ShapeDtypeStruct((T, 3 * D_GROUP), F32)],
        compiler_params=_params(("parallel",)),
        name="inproj_local_mixers",
    )(x2d, x2d, x2d, g, w, pool_w, pool_scale, sconv_w, hy_short_w)


def _ffn_body(x_ref, g_ref, wg_ref, wu_ref, wd_ref, gf_ref, o_ref, h_ref, *, final_norm):
    j = pl.program_id(1)

    @pl.when(j == 0)
    def _():
        x = x_ref[...]
        h_ref[...] = _rms(x, g_ref[...]).astype(BF16)
        o_ref[...] = x

    h = h_ref[...]
    gate = jnp.dot(h, wg_ref[...], preferred_element_type=F32)
    up = jnp.dot(h, wu_ref[...], preferred_element_type=F32)
    act = (gate * jax.nn.sigmoid(gate) * up).astype(BF16)
    o_ref[...] += jnp.dot(act, wd_ref[...], preferred_element_type=F32)

    if final_norm:
        @pl.when(j == pl.num_programs(1) - 1)
        def _():
            o_ref[...] = _rms(o_ref[...], gf_ref[...])


def _ffn(x2d, g, w_gate_up, w_down, gf, layer, final_norm):
    T = x2d.shape[0]
    tm, tf = 1024, 512
    nf = D_FF // tf
    return pl.pallas_call(
        functools.partial(_ffn_body, final_norm=final_norm),
        grid=(T // tm, nf),
        in_specs=[
            pl.BlockSpec((tm, D_MODEL), lambda i, j: (i, 0)),
            pl.BlockSpec((1, D_MODEL), lambda i, j: (0, 0)),
            pl.BlockSpec((None, D_MODEL, tf), lambda i, j: (layer, 0, j)),
            pl.BlockSpec((None, D_MODEL, tf), lambda i, j: (layer, 0, j + nf)),
            pl.BlockSpec((None, tf, D_MODEL), lambda i, j: (layer, j, 0)),
            pl.BlockSpec((1, D_MODEL), lambda i, j: (0, 0)),
        ],
        out_specs=pl.BlockSpec((tm, D_MODEL), lambda i, j: (i, 0)),
        out_shape=jax.ShapeDtypeStruct((T, D_MODEL), F32),
        scratch_shapes=[pltpu.VMEM((tm, D_MODEL), BF16)],
        compiler_params=_params(("parallel", "arbitrary")),
        name="ffn",
    )(x2d, g, w_gate_up, w_gate_up, w_down, gf)


def _rope_tables(seq):
    inv_freq = ROPE_THETA ** (-np.arange(0, ROPE_DIM, 2, dtype=np.float64) / ROPE_DIM)
    ang = np.arange(seq, dtype=np.float64)[:, None] * inv_freq[None, :]
    half = ROPE_DIM // 2
    cos = np.ones((seq, HEAD_DIM))
    s_lo = np.zeros((seq, HEAD_DIM))
    s_hi = np.zeros((seq, HEAD_DIM))
    cos[:, :half] = np.cos(ang)
    cos[:, half:ROPE_DIM] = np.cos(ang)
    s_lo[:, half:ROPE_DIM] = np.sin(ang)
    s_hi[:, :half] = -np.sin(ang)
    rep = LANES // HEAD_DIM
    return tuple(jnp.asarray(np.tile(a, (1, rep)), F32) for a in (cos, s_lo, s_hi))


LOG2E = math.log2(math.e)
ATTN_QBLOCKS = 4


def _rope(x, cos, s_lo, s_hi):
    half = ROPE_DIM // 2
    return x * cos + pltpu.roll(x, half, axis=1) * s_lo + pltpu.roll(x, LANES - half, axis=1) * s_hi


def _attn_tile(q_ref, kp_ref, kc_ref, kn_ref, vp_ref, vc_ref, vn_ref, cq_ref, cp_ref, cn_ref, sink_ref, n, nstep):
    two = 2 * BLOCK
    blk_rows = lambda i: slice(i * BLOCK, (i + 1) * BLOCK)
    tq = tuple(r[...] for r in cq_ref)

    q = q_ref[...]
    qscale = HEAD_DIM ** -0.5 * LOG2E
    qg = [(_rope(q[:, g * LANES:(g + 1) * LANES], *tq) * qscale).astype(BF16) for g in range(D_GROUP // LANES)]

    lane = lax.broadcasted_iota(jnp.int32, (BLOCK, LANES), 1)
    first = lane < HEAD_DIM
    half_ones = jnp.where(first, 1.0, 0.0)
    ones_bd = jnp.concatenate([half_ones, 1.0 - half_ones], axis=0)

    def per_head(x):
        xr = pltpu.roll(x, HEAD_DIM, axis=1)
        return [jnp.concatenate([jnp.where(first, x, 0.0), jnp.where(first, 0.0, xr)], axis=0),
                jnp.concatenate([jnp.where(first, xr, 0.0), jnp.where(first, 0.0, x)], axis=0)]

    kc = _rope(kc_ref[...], *tq)
    vc = vc_ref[...]
    kblocks = ([_rope(kp_ref[...], *(r[...] for r in cp_ref))] + [kc[blk_rows(i)] for i in range(ATTN_QBLOCKS)]
               + [_rope(kn_ref[...], *(r[...] for r in cn_ref))])
    vblocks = [vp_ref[...]] + [vc[blk_rows(i)] for i in range(ATTN_QBLOCKS)] + [vn_ref[...]]
    kcat = [[a.astype(BF16) for a in per_head(k)] for k in kblocks]
    vcat = [[jnp.concatenate([a, ones_bd], axis=1).astype(BF16) for a in per_head(v)] for v in vblocks]

    r = lax.broadcasted_iota(jnp.int32, (two, two), 0) % BLOCK
    c = lax.broadcasted_iota(jnp.int32, (two, two), 1)
    left = c < BLOCK
    c = c % BLOCK
    below, above = c >= r, c <= r
    first2 = lax.broadcasted_iota(jnp.int32, (two, LANES), 1) < HEAD_DIM

    tiles = []
    for i in range(ATTN_QBLOCKS):
        lo_mask = (below & (n > 0)) if i == 0 else below
        hi_mask = (above & (n < nstep - 1)) if i == ATTN_QBLOCKS - 1 else above
        masks = [lo_mask, None, hi_mask]
        outs = []
        for kv in range(N_KV_HEADS):
            qs = jnp.concatenate([qg[2 * kv][blk_rows(i)], qg[2 * kv + 1][blk_rows(i)]], axis=0)
            s = []
            for j in range(3):
                sb = lax.dot_general(qs, kcat[i + j][kv], (((1,), (1,)), ((), ())), preferred_element_type=F32)
                if masks[j] is not None:
                    sb = jnp.where(masks[j], sb, -jnp.inf)
                s.append(sb)
            sink = sink_ref[kv]
            top = jnp.maximum(jnp.maximum(s[0], s[1]), s[2])
            m = jnp.where(left, jnp.max(top[:, :BLOCK], axis=-1, keepdims=True),
                          jnp.max(top[:, BLOCK:], axis=-1, keepdims=True))
            m = jnp.maximum(m, sink)
            acc = sum(jnp.dot(jnp.exp2(s[j] - m).astype(BF16), vcat[i + j][kv], preferred_element_type=F32)
                      for j in range(3))
            m_out = jnp.where(first2, m[:, :LANES], m[:, LANES:])
            sink_out = jnp.where(first2, sink[:, :LANES], sink[:, LANES:])
            o = acc[:, :LANES] / (acc[:, LANES:] + jnp.exp2(sink_out - m_out))
            outs += [o[:BLOCK], o[BLOCK:]]
        tiles.append(jnp.concatenate(outs, axis=1).astype(BF16))
    return jnp.concatenate(tiles, axis=0)


def _attn_outproj_body(x_ref, a_ref, c_ref, d_ref, w_ref, *rest, nstep):
    attn_refs, o_ref = rest[:-1], rest[-1]
    acc = x_ref[...]
    for k, m_ref in ((0, a_ref), (2, c_ref), (3, d_ref)):
        acc = acc + jnp.dot(m_ref[...], w_ref[k * D_GROUP:(k + 1) * D_GROUP, :], preferred_element_type=F32)
    b = _attn_tile(*attn_refs, pl.program_id(0) % nstep, nstep)
    o_ref[...] = acc + jnp.dot(b, w_ref[D_GROUP:2 * D_GROUP, :], preferred_element_type=F32)


def _attn_outproj(x2d, qkv, sink, out_a, out_c, out_d, w, layer, seq):
    T = x2d.shape[0]
    rows = ATTN_QBLOCKS * BLOCK
    nblk = seq // BLOCK
    nstep = seq // rows
    last_blk = T // BLOCK - 1
    tabs = _rope_tables(seq)
    sink_tab = jnp.repeat(jnp.repeat((sink.astype(F32) * LOG2E).reshape(N_KV_HEADS, 2, 2), BLOCK, axis=1),
                          BLOCK, axis=2)
    kcol, vcol = (OFF_K - OFF_Q) // LANES, (OFF_V - OFF_Q) // LANES

    prev = lambda i: jnp.maximum(i * ATTN_QBLOCKS - 1, 0)
    nxt = lambda i: jnp.minimum((i + 1) * ATTN_QBLOCKS, last_blk)

    def own(width, col=0):
        return pl.BlockSpec((rows, width), lambda i: (i, col))

    def edge(col, sel):
        return pl.BlockSpec((BLOCK, LANES), lambda i: (sel(i), col))

    def tab_edge(sel):
        return [pl.BlockSpec((BLOCK, LANES), lambda i: (sel(i) % nblk, 0))] * 3

    return pl.pallas_call(
        functools.partial(_attn_outproj_body, nstep=nstep),
        grid=(T // rows,),
        in_specs=[
            own(D_MODEL), own(D_GROUP), own(D_GROUP), own(D_GROUP),
            pl.BlockSpec((None, D_MODEL, D_MODEL), lambda i: (layer, 0, 0), pipeline_mode=pl.Buffered(1)),
            own(D_GROUP),
            edge(kcol, prev), own(LANES, kcol), edge(kcol, nxt),
            edge(vcol, prev), own(LANES, vcol), edge(vcol, nxt),
            [pl.BlockSpec((rows, LANES), lambda i: (i % nstep, 0))] * 3, tab_edge(prev), tab_edge(nxt),
            pl.BlockSpec((N_KV_HEADS, 2 * BLOCK, 2 * BLOCK), lambda i: (0, 0, 0)),
        ],
        out_specs=own(D_MODEL),
        out_shape=jax.ShapeDtypeStruct((T, D_MODEL), F32),
        compiler_params=_params(("parallel",)),
        name="attention_outproj",
    )(x2d, out_a, out_c, out_d, w, qkv, qkv, qkv, qkv, qkv, qkv, qkv, list(tabs), list(tabs), list(tabs), sink_tab)


def _is(x, v):
    return abs(x - v) < 1e-12


def _scale(a, s):
    if a is None or _is(s, 0.0):
        return None
    if _is(s, 1.0):
        return a
    if _is(s, -1.0):
        return -a
    return a * s


def _cmul_const(a, w):
    ar, ai = a
    return (_psub(_scale(ar, w.real), _scale(ai, w.imag)), _padd(_scale(ar, w.imag), _scale(ai, w.real)))


def _padd(x, y):
    if x is None:
        return y
    if y is None:
        return x
    return x + y


def _psub(x, y):
    if y is None:
        return x
    if x is None:
        return -y
    return x - y


def _fft_blocks(xs, sign):
    n = len(xs)
    if n == 1:
        return list(xs)
    ev = _fft_blocks(xs[0::2], sign)
    od = _fft_blocks(xs[1::2], sign)
    out = [None] * n
    for k in range(n // 2):
        t = _cmul_const(od[k], np.exp(sign * 2j * math.pi * k / n))
        out[k] = (_padd(ev[k][0], t[0]), _padd(ev[k][1], t[1]))
        out[k + n // 2] = (_psub(ev[k][0], t[0]), _psub(ev[k][1], t[1]))
    return out


def _dense(a, like):
    return jnp.zeros_like(like) if a is None else a


def _dft_stack():
    k = np.arange(DFT_N, dtype=np.float64)
    ang = 2.0 * math.pi * np.outer(k, k) / DFT_N
    return jnp.asarray(np.concatenate([np.cos(ang), -np.sin(ang)], axis=0), F32)


def _twiddle_table(n1f):
    f1 = np.arange(n1f, dtype=np.float64)[:, None]
    t2 = np.arange(DFT_N, dtype=np.float64)[None, :]
    ang = (-2.0 * math.pi / (n1f * DFT_N)) * f1 * t2
    rep = lambda a: jnp.asarray(np.repeat(a.reshape(-1, 1), LANES, axis=1), F32)
    return rep(np.cos(ang)), rep(np.sin(ang))


def _block_dft(fst, qr, qi):
    rhs = jnp.concatenate([qr, qi], axis=1).astype(BF16)
    d = jnp.dot(fst, rhs, preferred_element_type=F32)
    return d[0:DFT_N, 0:LANES], d[0:DFT_N, LANES:], d[DFT_N:, 0:LANES], d[DFT_N:, LANES:]


def _forward_rows(xs, n1f):
    if len(xs) == n1f:
        return _fft_blocks(xs, -1.0)
    ev = _fft_blocks(xs, -1.0)
    od = _fft_blocks([_cmul_const(x, np.exp(-2j * math.pi * t / n1f)) for t, x in enumerate(xs)], -1.0)
    out = [None] * n1f
    out[0::2] = ev
    out[1::2] = od
    return out


MID_BLOCKS = 4


def _block_rows(f):
    return pl.ds(pl.multiple_of(f * DFT_N, DFT_N), DFT_N)


def _store_rows(q_scr, blocks, r, like):
    for f, (pr, pi) in enumerate(blocks):
        rows = pl.ds(f * DFT_N + r, SUBLANES)
        q_scr[rows, 0:LANES] = _dense(pr, like)
        q_scr[rows, LANES:] = _dense(pi, like)


def _twiddled_block_dft(fst, q_scr, twr_ref, twi_ref, rows):
    pr, pi = q_scr[rows, 0:LANES], q_scr[rows, LANES:]
    tr, ti = twr_ref[rows, :], twi_ref[rows, :]
    cr, ci, sr, si = _block_dft(fst, pr * tr - pi * ti, pr * ti + pi * tr)
    return cr - si, ci + sr


def _hconv_body(u_ref, g_ref, kr_ref, ki_ref, twr_ref, twi_ref, fst_ref, b_ref, o_ref, q_scr, *, seq):
    n1 = seq // DFT_N
    n1f = 2 * n1
    fst = fst_ref[...]

    def fwd(i, carry):
        r = pl.multiple_of(i * SUBLANES, SUBLANES)
        xs = [(u_ref[pl.ds(t * DFT_N + r, SUBLANES), :], u_ref[pl.ds(seq + t * DFT_N + r, SUBLANES), :])
              for t in range(n1)]
        _store_rows(q_scr, _forward_rows(xs, n1f), r, xs[0][0])
        return carry

    lax.fori_loop(0, DFT_N // SUBLANES, fwd, 0)

    def mid(i, carry):
        rows = [_block_rows(i * MID_BLOCKS + k) for k in range(MID_BLOCKS)]
        x = [_twiddled_block_dft(fst, q_scr, twr_ref, twi_ref, r) for r in rows]
        y = []
        for r, (xr, xi) in zip(rows, x):
            kr, ki = kr_ref[r, :].astype(F32), ki_ref[r, :].astype(F32)
            y.append((xr * kr - xi * ki, xr * ki + xi * kr))
        e = [_block_dft(fst, yr, yi) for yr, yi in y]
        s = []
        for r, (cr, ci, sr, si) in zip(rows, e):
            rr, ri = cr + si, ci - sr
            tr, ti = twr_ref[r, :], twi_ref[r, :]
            s.append((rr * tr + ri * ti, ri * tr - rr * ti))
        for r, (sr_, si_) in zip(rows, s):
            q_scr[r, 0:LANES] = sr_
            q_scr[r, LANES:] = si_
        return carry

    lax.fori_loop(0, n1f // MID_BLOCKS, mid, 0)

    bias = b_ref[...]

    def inv(i, carry):
        r = pl.multiple_of(i * SUBLANES, SUBLANES)
        ss = [(q_scr[pl.ds(f * DFT_N + r, SUBLANES), 0:LANES], q_scr[pl.ds(f * DFT_N + r, SUBLANES), LANES:])
              for f in range(n1f)]
        ev = _fft_blocks(ss[0::2], 1.0)
        od = _fft_blocks(ss[1::2], 1.0)
        for t in range(n1):
            o = _cmul_const(od[t], np.exp(2j * math.pi * t / n1f))
            for b, y in enumerate((ev[t][0] + o[0], ev[t][1] + o[1])):
                rows = pl.ds(b * seq + t * DFT_N + r, SUBLANES)
                o_ref[rows, :] = (g_ref[rows, :] * (y + bias * u_ref[rows, :])).astype(o_ref.dtype)
        return carry

    lax.fori_loop(0, DFT_N // SUBLANES, inv, 0)


def _hconv(u, ucol, g, gcol, spec_r, spec_i, order, tw, fst, bias, nb, seq, out_dtype):
    n1f = 2 * seq // DFT_N
    nc = D_GROUP // LANES
    once = pl.Buffered(1)
    return pl.pallas_call(
        functools.partial(_hconv_body, seq=seq),
        grid=(nc, nb // 2),
        in_specs=[
            pl.BlockSpec((2 * seq, LANES), lambda c, p: (p, ucol + c)),
            pl.BlockSpec((2 * seq, LANES), lambda c, p: (p, gcol + c)),
            pl.BlockSpec((None, n1f * DFT_N, LANES), lambda c, p: (order, 0, c)),
            pl.BlockSpec((None, n1f * DFT_N, LANES), lambda c, p: (order, 0, c)),
            pl.BlockSpec((n1f * DFT_N, LANES), lambda c, p: (0, 0), pipeline_mode=once),
            pl.BlockSpec((n1f * DFT_N, LANES), lambda c, p: (0, 0), pipeline_mode=once),
            pl.BlockSpec((2 * DFT_N, DFT_N), lambda c, p: (0, 0)),
            pl.BlockSpec((1, LANES), lambda c, p: (0, c)),
        ],
        out_specs=pl.BlockSpec((2 * seq, LANES), lambda c, p: (p, c)),
        out_shape=jax.ShapeDtypeStruct((nb * seq, D_GROUP), out_dtype),
        scratch_shapes=[pltpu.VMEM((n1f * DFT_N, 2 * LANES), F32)],
        compiler_params=_params(("arbitrary", "arbitrary")),
        name="hyena_long_conv",
    )(u, g, spec_r, spec_i, tw[0], tw[1], fst, bias[order][None, :])


def _hidden_body(f_ref, w1_ref, b1_ref, w2_ref, b2_ref, fr_ref, o_ref):
    hi = lax.Precision.HIGHEST
    fr = fr_ref[...]
    h = jnp.sin(fr * (jnp.dot(f_ref[...], w1_ref[...], precision=hi, preferred_element_type=F32) + b1_ref[...]))
    o_ref[...] = jnp.sin(fr * (jnp.dot(h, w2_ref[...], precision=hi, preferred_element_type=F32) + b2_ref[...]))


def _filter_features(seq):
    bands = np.linspace(1e-4, HY_BANDS - 1, HY_BANDS)[None, :]

    def feats(j):
        wpos = (2.0 * math.pi / seq) * j
        return np.concatenate([j / (seq - 1.0), np.cos(bands * wpos), -np.sin(bands * wpos)], axis=-1)

    j = np.arange(seq, dtype=np.float64)[:, None]
    return jnp.asarray(np.concatenate([feats(j), feats(seq - j)], axis=1), F32)


def _blockdiag2(w):
    z = jnp.zeros_like(w)
    return jnp.concatenate([jnp.concatenate([w, z], axis=1), jnp.concatenate([z, w], axis=1)], axis=0)


def _filter_hidden(feats, w1, b1, w2, b2, freq):
    seq = feats.shape[0]
    two = lambda v: jnp.concatenate([v, v])[None, :]
    tr = 512
    full = lambda a: pl.BlockSpec(a.shape, lambda i: (0, 0))
    args = (_blockdiag2(w1), two(b1), _blockdiag2(w2), two(b2), two(freq))
    return pl.pallas_call(
        _hidden_body,
        grid=(seq // tr,),
        in_specs=[pl.BlockSpec((tr, 2 * HY_EMB), lambda i: (i, 0))] + [full(a) for a in args],
        out_specs=pl.BlockSpec((tr, 2 * HY_HIDDEN), lambda i: (i, 0)),
        out_shape=jax.ShapeDtypeStruct((seq, 2 * HY_HIDDEN), F32),
        compiler_params=_params(("parallel",)),
        name="hyena_filter_hidden",
    )(feats, *args)


FILTER_ROWS = 512


def _split_bf16(a):
    hi = a.astype(BF16)
    return hi, (a - hi.astype(F32)).astype(BF16)


def _hfilter_body(h_ref, w3_ref, d_ref, twr_ref, twi_ref, fst_ref, or_ref, oi_ref, k_scr, q_scr, *, seq):
    n1f = 2 * seq // DFT_N
    dec = jnp.abs(d_ref[...])
    w_hi, w_lo = _split_bf16(w3_ref[...])
    norm = jnp.zeros((1, LANES), F32)
    for r0 in range(0, seq, FILTER_ROWS):
        j = (lax.broadcasted_iota(jnp.int32, (FILTER_ROWS, 1), 0) + r0).astype(F32)
        h_hi, h_lo = _split_bf16(h_ref[r0:r0 + FILTER_ROWS, :])
        k2 = (jnp.dot(h_hi, w_hi, preferred_element_type=F32) + jnp.dot(h_hi, w_lo, preferred_element_type=F32)
              + jnp.dot(h_lo, w_hi, preferred_element_type=F32))
        kf = k2[:, :LANES] * jnp.exp(-(j / (seq - 1.0)) * dec[:, :LANES])
        kb = k2[:, LANES:] * jnp.exp(-((seq - j) / (seq - 1.0)) * dec[:, LANES:])
        kb = jnp.where(j > 0.0, kb, 0.0)
        norm = norm + jnp.sum(jnp.abs(kf), axis=0, keepdims=True) + jnp.sum(jnp.abs(kb), axis=0, keepdims=True)
        k_scr[r0:r0 + FILTER_ROWS, :] = kf
        k_scr[seq + r0:seq + r0 + FILTER_ROWS, :] = kb
    scale = (1.0 / (2 * seq)) / norm
    for r0 in range(0, 2 * seq, FILTER_ROWS):
        k_scr[r0:r0 + FILTER_ROWS, :] = k_scr[r0:r0 + FILTER_ROWS, :] * scale
    fst = fst_ref[...]

    def fwd(i, carry):
        r = pl.multiple_of(i * SUBLANES, SUBLANES)
        xs = [(k_scr[pl.ds(t1 * DFT_N + r, SUBLANES), :], None) for t1 in range(n1f)]
        _store_rows(q_scr, _forward_rows(xs, n1f), r, xs[0][0])
        return carry

    lax.fori_loop(0, DFT_N // SUBLANES, fwd, 0)

    def mid(i, carry):
        rows = [_block_rows(i * MID_BLOCKS + k) for k in range(MID_BLOCKS)]
        x = [_twiddled_block_dft(fst, q_scr, twr_ref, twi_ref, r) for r in rows]
        for r, (xr, xi) in zip(rows, x):
            or_ref[r, :] = xr.astype(or_ref.dtype)
            oi_ref[r, :] = xi.astype(oi_ref.dtype)
        return carry

    lax.fori_loop(0, n1f // MID_BLOCKS, mid, 0)


def _by_block(a):
    nc = D_GROUP // LANES
    lead = a.shape[:-1]
    a = a.reshape(lead + (HY_ORDER, 2, nc, LANES))
    return jnp.swapaxes(a, -3, -2).reshape(lead + (HY_ORDER * nc * 2 * LANES,))


def _hfilter(hid, w3, decay, tw, fst, seq):
    n1f = 2 * seq // DFT_N
    nc = D_GROUP // LANES
    direction = jnp.arange(w3.shape[1]) // D_GROUP % 2
    w3 = jnp.concatenate([jnp.where(direction == 0, w3, 0.0), jnp.where(direction == 1, w3, 0.0)], axis=0)
    w3 = _by_block(w3)
    dec = _by_block(decay)[None, :]
    once = pl.Buffered(1)
    pair = lambda o, c: (0, nc * o + c)
    out_spec = pl.BlockSpec((None, n1f * DFT_N, LANES), lambda o, c: (o, 0, c))
    out_shape = jax.ShapeDtypeStruct((HY_ORDER, n1f * DFT_N, D_GROUP), BF16)
    return pl.pallas_call(
        functools.partial(_hfilter_body, seq=seq),
        grid=(HY_ORDER, nc),
        in_specs=[
            pl.BlockSpec((seq, 2 * HY_HIDDEN), lambda o, c: (0, 0)),
            pl.BlockSpec((2 * HY_HIDDEN, 2 * LANES), pair),
            pl.BlockSpec((1, 2 * LANES), pair),
            pl.BlockSpec((n1f * DFT_N, LANES), lambda o, c: (0, 0), pipeline_mode=once),
            pl.BlockSpec((n1f * DFT_N, LANES), lambda o, c: (0, 0), pipeline_mode=once),
            pl.BlockSpec((2 * DFT_N, DFT_N), lambda o, c: (0, 0)),
        ],
        out_specs=[out_spec, out_spec],
        out_shape=[out_shape, out_shape],
        scratch_shapes=[pltpu.VMEM((2 * seq, LANES), F32), pltpu.VMEM((n1f * DFT_N, 2 * LANES), F32)],
        compiler_params=_params(("arbitrary", "arbitrary")),
        name="hyena_filter_spectrum",
    )(hid, w3, dec, tw[0], tw[1], fst)


def _trunk(x, p, fst):
    nb, seq, _ = x.shape
    x2d = x.reshape(nb * seq, D_MODEL)
    tw = _twiddle_table(2 * seq // DFT_N)
    feats = _filter_features(seq)
    vcol = 0
    g1col = D_GROUP // LANES
    g2col = 2 * D_GROUP // LANES
    for l in range(DEPTH):
        qkv, out_a, out_c, hz = _inmix(x2d, p["mix_norm_g"][l][None, :], p["w_in"], p["pool_w"],
                                       p["pool_scale"][l][None, :], p["sconv_w"][l], p["hy_short_w"][l], l, seq)
        hid = _filter_hidden(feats, p["hy_w1"][l], p["hy_b1"][l], p["hy_w2"][l], p["hy_b2"][l], p["hy_freq"][l])
        spec_r, spec_i = _hfilter(hid, p["hy_w3"][l], p["hy_decay"][l], tw, fst, seq)
        bias = p["hy_bias"][l]
        u1 = _hconv(hz, vcol, hz, g1col, spec_r, spec_i, 0, tw, fst, bias, nb, seq, F32)
        out_d = _hconv(u1, 0, hz, g2col, spec_r, spec_i, 1, tw, fst, bias, nb, seq, BF16)
        x2d = _attn_outproj(x2d, qkv, p["attn_sink"][l], out_a, out_c, out_d, p["w_out"], l, seq)
        x2d = _ffn(x2d, p["ffn_norm_g"][l][None, :], p["w_gate_up"], p["w_down"],
                   p["final_norm_g"][None, :], l, final_norm=(l == DEPTH - 1))
    return x2d.reshape(nb, seq, D_MODEL)


def kernel(x_prompt, x_sample, mix_norm_g, w_in, pool_w, pool_scale, attn_sink, sconv_w, hy_short_w, hy_w1, hy_b1, hy_w2, hy_b2, hy_w3, hy_freq, hy_decay, hy_bias, w_out, ffn_norm_g, w_gate_up, w_down, final_norm_g):
    p = dict(
        mix_norm_g=mix_norm_g, w_in=w_in.astype(BF16), pool_w=pool_w.astype(BF16), pool_scale=pool_scale,
        attn_sink=attn_sink, sconv_w=sconv_w, hy_short_w=hy_short_w, hy_w1=hy_w1, hy_b1=hy_b1, hy_w2=hy_w2,
        hy_b2=hy_b2, hy_w3=hy_w3, hy_freq=hy_freq, hy_decay=hy_decay, hy_bias=hy_bias,
        w_out=w_out.astype(BF16), ffn_norm_g=ffn_norm_g, w_gate_up=w_gate_up.astype(BF16),
        w_down=w_down.astype(BF16), final_norm_g=final_norm_g,
    )
    fst = _dft_stack().astype(BF16)
    return (_trunk(x_prompt, p, fst), _trunk(x_sample, p, fst))
```

```python
import functools
import math

import numpy as np
import jax
import jax.numpy as jnp
from jax import lax
from jax.experimental import pallas as pl
from jax.experimental.pallas import tpu as pltpu

F32 = jnp.float32
BF16 = jnp.bfloat16

D_MODEL = 2048
DEPTH = 2
D_GROUP = 512
POOL_WINDOWS = (2, 4, 8, 16)
POOL_CH = 128
HEAD_DIM = 64
N_Q_HEADS = 8
N_KV_HEADS = 2
GQA_GROUP = 4
BLOCK = 128
ROPE_DIM = 16
ROPE_THETA = 500000.0
HY_ORDER = 2
HY_EMB = 33
HY_BANDS = 16
HY_HIDDEN = 64
D_FF = 5632
NORM_EPS = 1e-6
OFF_Q = 512
OFF_K = 1024
OFF_V = 1152
OFF_C = 1280
OFF_D = 2816
D_IN = 4352

LANES = 128
SUBLANES = 8
DFT_N = 256
VMEM_LIMIT = 56 * 2 ** 20


def _params(sem, vmem=VMEM_LIMIT):
    return pltpu.CompilerParams(dimension_semantics=sem, vmem_limit_bytes=vmem)


def _rms(x, g):
    ms = jnp.mean(x * x, axis=-1, keepdims=True)
    return x * lax.rsqrt(ms + NORM_EPS) * g


HALO = SUBLANES
D_QKV = OFF_C - OFF_Q


def _conv3_ext(z, w_ref):
    n = z.shape[0]
    return (pltpu.roll(z, 1, axis=0) * w_ref[0:1, :] + z * w_ref[1:2, :]
            + pltpu.roll(z, n - 1, axis=0) * w_ref[2:3, :])


def _inmix_body(x_ref, xp_ref, xn_ref, g_ref, w_ref, pw_ref, ps_ref, cw_ref, hw_ref,
                qkv_ref, a_ref, c_ref, hz_ref, *, seq):
    tm = x_ref.shape[0]
    ext = tm + 2 * HALO
    core = slice(HALO, HALO + tm)
    g = g_ref[...]
    hn = jnp.concatenate([_rms(xp_ref[...], g), _rms(x_ref[...], g), _rms(xn_ref[...], g)], axis=0).astype(BF16)

    pos0 = (pl.program_id(0) * tm) % seq
    row = lax.broadcasted_iota(jnp.int32, (ext, 1), 0)
    inside = ((row >= HALO) | (pos0 > 0)) & ((row < HALO + tm) | (pos0 + tm < seq))

    def project(lo, hi):
        z = jnp.dot(hn, w_ref[:, lo:hi], preferred_element_type=F32)
        return jnp.where(inside, z, 0.0)

    qkv_ref[...] = project(OFF_Q, OFF_C)[core]

    za = project(0, OFF_Q)
    pos = pos0 + row - HALO
    for k, win in enumerate(POOL_WINDOWS):
        cols = slice(k * POOL_CH, (k + 1) * POOL_CH)
        p = za[:, cols]
        s = p + pltpu.roll(p, 1, axis=0)
        step = 1
        while 2 * step < win:
            s = pltpu.roll(s, step, axis=0) + pltpu.roll(s, ext - step, axis=0)
            step *= 2
        half = win // 2
        cnt = jnp.maximum(jnp.minimum(pos + half, seq) - jnp.maximum(pos - half, 0), 1).astype(F32)
        d = (s / cnt - p)[core]
        out = jnp.dot(d.astype(BF16), pw_ref[k], preferred_element_type=F32) * ps_ref[:, cols]
        a_ref[:, cols] = out.astype(a_ref.dtype)

    zc = project(OFF_C, OFF_D)
    ch, cb, cc = zc[:, :D_GROUP], zc[:, D_GROUP:2 * D_GROUP], zc[:, 2 * D_GROUP:]
    c_ref[...] = (cb * _conv3_ext(cc * ch, cw_ref))[core].astype(c_ref.dtype)

    hz_ref[...] = _conv3_ext(project(OFF_D, D_IN), hw_ref)[core]


def _inmix(x2d, g, w, pool_w, pool_scale, sconv_w, hy_short_w, layer, seq):
    T = x2d.shape[0]
    tm = 512
    per = tm // HALO
    last = T // HALO - 1
    full = lambda a: pl.BlockSpec(a.shape, lambda i: (0, 0))
    row_tile = lambda width: pl.BlockSpec((tm, width), lambda i: (i, 0))
    return pl.pallas_call(
        functools.partial(_inmix_body, seq=seq),
        grid=(T // tm,),
        in_specs=[
            row_tile(D_MODEL),
            pl.BlockSpec((HALO, D_MODEL), lambda i: (jnp.maximum(i * per - 1, 0), 0)),
            pl.BlockSpec((HALO, D_MODEL), lambda i: (jnp.minimum((i + 1) * per, last), 0)),
            full(g),
            pl.BlockSpec((None, D_MODEL, D_IN), lambda i: (layer, 0, 0), pipeline_mode=pl.Buffered(1)),
            pl.BlockSpec((None, len(POOL_WINDOWS), POOL_CH, POOL_CH), lambda i: (layer, 0, 0, 0)),
            full(pool_scale), full(sconv_w), full(hy_short_w),
        ],
        out_specs=[row_tile(D_QKV), row_tile(D_GROUP), row_tile(D_GROUP), row_tile(3 * D_GROUP)],
        out_shape=[jax.ShapeDtypeStruct((T, D_QKV), F32), jax.ShapeDtypeStruct((T, D_GROUP), BF16),
                   jax.ShapeDtypeStruct((T, D_GROUP), BF16), jax.ShapeDtypeStruct((T, 3 * D_GROUP), F32)],
        compiler_params=_params(("parallel",)),
        name="inproj_local_mixers",
    )(x2d, x2d, x2d, g, w, pool_w, pool_scale, sconv_w, hy_short_w)


def _ffn_body(x_ref, g_ref, wg_ref, wu_ref, wd_ref, gf_ref, o_ref, h_ref, *, final_norm):
    j = pl.program_id(1)

    @pl.when(j == 0)
    def _():
        x = x_ref[...]
        h_ref[...] = _rms(x, g_ref[...]).astype(BF16)
        o_ref[...] = x

    h = h_ref[...]
    gate = jnp.dot(h, wg_ref[...], preferred_element_type=F32)
    up = jnp.dot(h, wu_ref[...], preferred_element_type=F32)
    act = (gate * jax.nn.sigmoid(gate) * up).astype(BF16)
    o_ref[...] += jnp.dot(act, wd_ref[...], preferred_element_type=F32)

    if final_norm:
        @pl.when(j == pl.num_programs(1) - 1)
        def _():
            o_ref[...] = _rms(o_ref[...], gf_ref[...])


def _ffn(x2d, g, w_gate_up, w_down, gf, layer, final_norm):
    T = x2d.shape[0]
    tm, tf = 1024, 512
    nf = D_FF // tf
    return pl.pallas_call(
        functools.partial(_ffn_body, final_norm=final_norm),
        grid=(T // tm, nf),
        in_specs=[
            pl.BlockSpec((tm, D_MODEL), lambda i, j: (i, 0)),
            pl.BlockSpec((1, D_MODEL), lambda i, j: (0, 0)),
            pl.BlockSpec((None, D_MODEL, tf), lambda i, j: (layer, 0, j)),
            pl.BlockSpec((None, D_MODEL, tf), lambda i, j: (layer, 0, j + nf)),
            pl.BlockSpec((None, tf, D_MODEL), lambda i, j: (layer, j, 0)),
            pl.BlockSpec((1, D_MODEL), lambda i, j: (0, 0)),
        ],
        out_specs=pl.BlockSpec((tm, D_MODEL), lambda i, j: (i, 0)),
        out_shape=jax.ShapeDtypeStruct((T, D_MODEL), F32),
        scratch_shapes=[pltpu.VMEM((tm, D_MODEL), BF16)],
        compiler_params=_params(("parallel", "arbitrary")),
        name="ffn",
    )(x2d, g, w_gate_up, w_gate_up, w_down, gf)


def _rope_tables(seq):
    inv_freq = ROPE_THETA ** (-np.arange(0, ROPE_DIM, 2, dtype=np.float64) / ROPE_DIM)
    ang = np.arange(seq, dtype=np.float64)[:, None] * inv_freq[None, :]
    half = ROPE_DIM // 2
    cos = np.ones((seq, HEAD_DIM))
    s_lo = np.zeros((seq, HEAD_DIM))
    s_hi = np.zeros((seq, HEAD_DIM))
    cos[:, :half] = np.cos(ang)
    cos[:, half:ROPE_DIM] = np.cos(ang)
    s_lo[:, half:ROPE_DIM] = np.sin(ang)
    s_hi[:, :half] = -np.sin(ang)
    rep = LANES // HEAD_DIM
    return tuple(jnp.asarray(np.tile(a, (1, rep)), F32) for a in (cos, s_lo, s_hi))


LOG2E = math.log2(math.e)
ATTN_QBLOCKS = 4


def _rope(x, cos, s_lo, s_hi):
    half = ROPE_DIM // 2
    return x * cos + pltpu.roll(x, half, axis=1) * s_lo + pltpu.roll(x, LANES - half, axis=1) * s_hi


def _attn_tile(q_ref, kp_ref, kc_ref, kn_ref, vp_ref, vc_ref, vn_ref, cq_ref, cp_ref, cn_ref, sink_ref, n, nstep):
    two = 2 * BLOCK
    blk_rows = lambda i: slice(i * BLOCK, (i + 1) * BLOCK)
    tq = tuple(r[...] for r in cq_ref)

    q = q_ref[...]
    qscale = HEAD_DIM ** -0.5 * LOG2E
    qg = [(_rope(q[:, g * LANES:(g + 1) * LANES], *tq) * qscale).astype(BF16) for g in range(D_GROUP // LANES)]

    lane = lax.broadcasted_iota(jnp.int32, (BLOCK, LANES), 1)
    first = lane < HEAD_DIM
    half_ones = jnp.where(first, 1.0, 0.0)
    ones_bd = jnp.concatenate([half_ones, 1.0 - half_ones], axis=0)

    def per_head(x):
        xr = pltpu.roll(x, HEAD_DIM, axis=1)
        return [jnp.concatenate([jnp.where(first, x, 0.0), jnp.where(first, 0.0, xr)], axis=0),
                jnp.concatenate([jnp.where(first, xr, 0.0), jnp.where(first, 0.0, x)], axis=0)]

    kc = _rope(kc_ref[...], *tq)
    vc = vc_ref[...]
    kblocks = ([_rope(kp_ref[...], *(r[...] for r in cp_ref))] + [kc[blk_rows(i)] for i in range(ATTN_QBLOCKS)]
               + [_rope(kn_ref[...], *(r[...] for r in cn_ref))])
    vblocks = [vp_ref[...]] + [vc[blk_rows(i)] for i in range(ATTN_QBLOCKS)] + [vn_ref[...]]
    kcat = [[a.astype(BF16) for a in per_head(k)] for k in kblocks]
    vcat = [[jnp.concatenate([a, ones_bd], axis=1).astype(BF16) for a in per_head(v)] for v in vblocks]

    r = lax.broadcasted_iota(jnp.int32, (two, two), 0) % BLOCK
    c = lax.broadcasted_iota(jnp.int32, (two, two), 1)
    left = c < BLOCK
    c = c % BLOCK
    below, above = c >= r, c <= r
    first2 = lax.broadcasted_iota(jnp.int32, (two, LANES), 1) < HEAD_DIM

    tiles = []
    for i in range(ATTN_QBLOCKS):
        lo_mask = (below & (n > 0)) if i == 0 else below
        hi_mask = (above & (n < nstep - 1)) if i == ATTN_QBLOCKS - 1 else above
        masks = [lo_mask, None, hi_mask]
        outs = []
        for kv in range(N_KV_HEADS):
            qs = jnp.concatenate([qg[2 * kv][blk_rows(i)], qg[2 * kv + 1][blk_rows(i)]], axis=0)
            s = []
            for j in range(3):
                sb = lax.dot_general(qs, kcat[i + j][kv], (((1,), (1,)), ((), ())), preferred_element_type=F32)
                if masks[j] is not None:
                    sb = jnp.where(masks[j], sb, -jnp.inf)
                s.append(sb)
            sink = sink_ref[kv]
            top = jnp.maximum(jnp.maximum(s[0], s[1]), s[2])
            m = jnp.where(left, jnp.max(top[:, :BLOCK], axis=-1, keepdims=True),
                          jnp.max(top[:, BLOCK:], axis=-1, keepdims=True))
            m = jnp.maximum(m, sink)
            acc = sum(jnp.dot(jnp.exp2(s[j] - m).astype(BF16), vcat[i + j][kv], preferred_element_type=F32)
                      for j in range(3))
            m_out = jnp.where(first2, m[:, :LANES], m[:, LANES:])
            sink_out = jnp.where(first2, sink[:, :LANES], sink[:, LANES:])
            o = acc[:, :LANES] / (acc[:, LANES:] + jnp.exp2(sink_out - m_out))
            outs += [o[:BLOCK], o[BLOCK:]]
        tiles.append(jnp.concatenate(outs, axis=1).astype(BF16))
    return jnp.concatenate(tiles, axis=0)


def _attn_outproj_body(x_ref, a_ref, c_ref, d_ref, w_ref, *rest, nstep):
    attn_refs, o_ref = rest[:-1], rest[-1]
    acc = x_ref[...]
    for k, m_ref in ((0, a_ref), (2, c_ref), (3, d_ref)):
        acc = acc + jnp.dot(m_ref[...], w_ref[k * D_GROUP:(k + 1) * D_GROUP, :], preferred_element_type=F32)
    b = _attn_tile(*attn_refs, pl.program_id(0) % nstep, nstep)
    o_ref[...] = acc + jnp.dot(b, w_ref[D_GROUP:2 * D_GROUP, :], preferred_element_type=F32)


def _attn_outproj(x2d, qkv, sink, out_a, out_c, out_d, w, layer, seq):
    T = x2d.shape[0]
    rows = ATTN_QBLOCKS * BLOCK
    nblk = seq // BLOCK
    nstep = seq // rows
    last_blk = T // BLOCK - 1
    tabs = _rope_tables(seq)
    sink_tab = jnp.repeat(jnp.repeat((sink.astype(F32) * LOG2E).reshape(N_KV_HEADS, 2, 2), BLOCK, axis=1),
                          BLOCK, axis=2)
    kcol, vcol = (OFF_K - OFF_Q) // LANES, (OFF_V - OFF_Q) // LANES

    prev = lambda i: jnp.maximum(i * ATTN_QBLOCKS - 1, 0)
    nxt = lambda i: jnp.minimum((i + 1) * ATTN_QBLOCKS, last_blk)

    def own(width, col=0):
        return pl.BlockSpec((rows, width), lambda i: (i, col))

    def edge(col, sel):
        return pl.BlockSpec((BLOCK, LANES), lambda i: (sel(i), col))

    def tab_edge(sel):
        return [pl.BlockSpec((BLOCK, LANES), lambda i: (sel(i) % nblk, 0))] * 3

    return pl.pallas_call(
        functools.partial(_attn_outproj_body, nstep=nstep),
        grid=(T // rows,),
        in_specs=[
            own(D_MODEL), own(D_GROUP), own(D_GROUP), own(D_GROUP),
            pl.BlockSpec((None, D_MODEL, D_MODEL), lambda i: (layer, 0, 0), pipeline_mode=pl.Buffered(1)),
            own(D_GROUP),
            edge(kcol, prev), own(LANES, kcol), edge(kcol, nxt),
            edge(vcol, prev), own(LANES, vcol), edge(vcol, nxt),
            [pl.BlockSpec((rows, LANES), lambda i: (i % nstep, 0))] * 3, tab_edge(prev), tab_edge(nxt),
            pl.BlockSpec((N_KV_HEADS, 2 * BLOCK, 2 * BLOCK), lambda i: (0, 0, 0)),
        ],
        out_specs=own(D_MODEL),
        out_shape=jax.ShapeDtypeStruct((T, D_MODEL), F32),
        compiler_params=_params(("parallel",)),
        name="attention_outproj",
    )(x2d, out_a, out_c, out_d, w, qkv, qkv, qkv, qkv, qkv, qkv, qkv, list(tabs), list(tabs), list(tabs), sink_tab)


def _is(x, v):
    return abs(x - v) < 1e-12


def _scale(a, s):
    if a is None or _is(s, 0.0):
        return None
    if _is(s, 1.0):
        return a
    if _is(s, -1.0):
        return -a
    return a * s


def _cmul_const(a, w):
    ar, ai = a
    return (_psub(_scale(ar, w.real), _scale(ai, w.imag)), _padd(_scale(ar, w.imag), _scale(ai, w.real)))


def _padd(x, y):
    if x is None:
        return y
    if y is None:
        return x
    return x + y


def _psub(x, y):
    if y is None:
        return x
    if x is None:
        return -y
    return x - y


def _fft_blocks(xs, sign, leaves_done=False):
    n = len(xs)
    if n == 1 or (n == 2 and leaves_done):
        return list(xs)
    ev = _fft_blocks(xs[0::2], sign, leaves_done)
    od = _fft_blocks(xs[1::2], sign, leaves_done)
    out = [None] * n
    for k in range(n // 2):
        t = _cmul_const(od[k], np.exp(sign * 2j * math.pi * k / n))
        out[k] = (_padd(ev[k][0], t[0]), _padd(ev[k][1], t[1]))
        out[k + n // 2] = (_psub(ev[k][0], t[0]), _psub(ev[k][1], t[1]))
    return out


def _dense(a, like):
    return jnp.zeros_like(like) if a is None else a


def _dft_stack():
    k = np.arange(DFT_N, dtype=np.float64)
    ang = 2.0 * math.pi * np.outer(k, k) / DFT_N
    c, s = np.cos(ang), np.sin(ang)
    return jnp.asarray(np.stack([np.concatenate([c, -s], axis=1), np.concatenate([c, s], axis=1)]), F32)


def _twiddle_table(n1f):
    f1 = np.arange(n1f, dtype=np.float64)[:, None]
    t2 = np.arange(DFT_N, dtype=np.float64)[None, :]
    ang = (-2.0 * math.pi / (n1f * DFT_N)) * f1 * t2
    rep = lambda a: jnp.asarray(np.repeat(a.reshape(-1, 1), LANES, axis=1), F32)
    return rep(np.cos(ang)), rep(np.sin(ang))


def _block_dft(f, qr, qi):
    qr, qi = qr.astype(BF16), qi.astype(BF16)
    rhs = jnp.concatenate([jnp.concatenate([qr, qi], axis=1), jnp.concatenate([-qi, qr], axis=1)], axis=0)
    d = jnp.dot(f, rhs, preferred_element_type=F32)
    return d[:, 0:LANES], d[:, LANES:]


def _forward_rows(xs, n1f):
    half = n1f // 2
    out = [None] * n1f
    if len(xs) == n1f:
        out[:half] = _fft_blocks(xs[0::2], -1.0)
        out[half:] = _fft_blocks(xs[1::2], -1.0)
        return out
    branches = (xs, [_cmul_const(x, np.exp(-2j * math.pi * t / n1f)) for t, x in enumerate(xs)])
    for b, xb in enumerate(branches):
        out[b:half:2] = _fft_blocks(xb[0::2], -1.0)
        out[half + b::2] = _fft_blocks(xb[1::2], -1.0)
    return out


def _pair_factors(n1f, pruned):
    f = np.arange(n1f // 2, dtype=np.float64)
    ang = -2.0 * math.pi * ((f // 2) / (n1f // 2) if pruned else f / n1f)
    return jnp.asarray(np.cos(ang), F32), jnp.asarray(np.sin(ang), F32)


MID_BLOCKS = 4
FILTER_MID_BLOCKS = 8


def _block_rows(f):
    start = f * DFT_N
    return pl.ds(start if isinstance(start, int) else pl.multiple_of(start, DFT_N), DFT_N)


def _store_rows(q_scr, blocks, r, like):
    for f, (pr, pi) in enumerate(blocks):
        rows = pl.ds(f * DFT_N + r, SUBLANES)
        q_scr[rows, 0:LANES] = _dense(pr, like)
        q_scr[rows, LANES:] = _dense(pi, like)


def _pair_spectra(fst, q_scr, twr_ref, twi_ref, cr_ref, ci_ref, f, half):
    lo, hi = _block_rows(f), _block_rows(f + half)
    er, ei = q_scr[lo, 0:LANES], q_scr[lo, LANES:]
    orr, oi = q_scr[hi, 0:LANES], q_scr[hi, LANES:]
    cr, ci = cr_ref[f], ci_ref[f]
    wr, wi = orr * cr - oi * ci, orr * ci + oi * cr
    out = []
    for rows, pr, pi in ((lo, er + wr, ei + wi), (hi, er - wr, ei - wi)):
        tr, ti = twr_ref[rows, :], twi_ref[rows, :]
        out.append((rows, _block_dft(fst[0], pr * tr - pi * ti, pr * ti + pi * tr)))
    return out


def _hconv_body(u_ref, g_ref, kr_ref, ki_ref, twr_ref, twi_ref, cr_ref, ci_ref, fst_ref, b_ref, o_ref, q_scr,
                *, seq):
    n1 = seq // DFT_N
    n1f = 2 * n1
    fst = fst_ref[...]

    def fwd(i, carry):
        r = pl.multiple_of(i * SUBLANES, SUBLANES)
        xs = [(u_ref[pl.ds(t * DFT_N + r, SUBLANES), :], u_ref[pl.ds(seq + t * DFT_N + r, SUBLANES), :])
              for t in range(n1)]
        _store_rows(q_scr, _forward_rows(xs, n1f), r, xs[0][0])
        return carry

    lax.fori_loop(0, DFT_N // SUBLANES, fwd, 0)

    def mid(i, carry):
        pairs = [_pair_spectra(fst, q_scr, twr_ref, twi_ref, cr_ref, ci_ref, i * (MID_BLOCKS // 2) + k, n1)
                 for k in range(MID_BLOCKS // 2)]
        s = []
        for pair in pairs:
            sp = []
            for r, (xr, xi) in pair:
                kr, ki = kr_ref[r, :].astype(F32), ki_ref[r, :].astype(F32)
                rr, ri = _block_dft(fst[1], xr * kr - xi * ki, xr * ki + xi * kr)
                tr, ti = twr_ref[r, :], twi_ref[r, :]
                sp.append((rr * tr + ri * ti, ri * tr - rr * ti))
            (lo, _), (hi, _) = pair
            s.append((lo, sp[0][0] + sp[1][0], sp[0][1] + sp[1][1]))
            s.append((hi, sp[0][0] - sp[1][0], sp[0][1] - sp[1][1]))
        for r, sr_, si_ in s:
            q_scr[r, 0:LANES] = sr_
            q_scr[r, LANES:] = si_
        return carry

    lax.fori_loop(0, n1f // MID_BLOCKS, mid, 0)

    bias = b_ref[...]

    def inv(i, carry):
        r = pl.multiple_of(i * SUBLANES, SUBLANES)
        ss = [(q_scr[pl.ds(f * DFT_N + r, SUBLANES), 0:LANES], q_scr[pl.ds(f * DFT_N + r, SUBLANES), LANES:])
              for f in range(n1f)]
        ev = _fft_blocks(ss[0::2], 1.0, leaves_done=True)
        od = _fft_blocks(ss[1::2], 1.0, leaves_done=True)
        for t in range(n1):
            o = _cmul_const(od[t], np.exp(2j * math.pi * t / n1f))
            for b, y in enumerate((ev[t][0] + o[0], ev[t][1] + o[1])):
                rows = pl.ds(b * seq + t * DFT_N + r, SUBLANES)
                o_ref[rows, :] = (g_ref[rows, :] * (y + bias * u_ref[rows, :])).astype(o_ref.dtype)
        return carry

    lax.fori_loop(0, DFT_N // SUBLANES, inv, 0)


def _hconv(u, ucol, g, gcol, spec_r, spec_i, order, tw, fst, bias, nb, seq, out_dtype):
    n1f = 2 * seq // DFT_N
    nc = D_GROUP // LANES
    once = pl.Buffered(1)
    return pl.pallas_call(
        functools.partial(_hconv_body, seq=seq),
        grid=(nc, nb // 2),
        in_specs=[
            pl.BlockSpec((2 * seq, LANES), lambda c, p: (p, ucol + c)),
            pl.BlockSpec((2 * seq, LANES), lambda c, p: (p, gcol + c)),
            pl.BlockSpec((None, n1f * DFT_N, LANES), lambda c, p: (order, 0, c)),
            pl.BlockSpec((None, n1f * DFT_N, LANES), lambda c, p: (order, 0, c)),
            pl.BlockSpec((n1f * DFT_N, LANES), lambda c, p: (0, 0), pipeline_mode=once),
            pl.BlockSpec((n1f * DFT_N, LANES), lambda c, p: (0, 0), pipeline_mode=once),
            pl.BlockSpec(memory_space=pltpu.SMEM), pl.BlockSpec(memory_space=pltpu.SMEM),
            pl.BlockSpec((2, DFT_N, 2 * DFT_N), lambda c, p: (0, 0, 0)),
            pl.BlockSpec((1, LANES), lambda c, p: (0, c)),
        ],
        out_specs=pl.BlockSpec((2 * seq, LANES), lambda c, p: (p, c)),
        out_shape=jax.ShapeDtypeStruct((nb * seq, D_GROUP), out_dtype),
        scratch_shapes=[pltpu.VMEM((n1f * DFT_N, 2 * LANES), F32)],
        compiler_params=_params(("arbitrary", "arbitrary")),
        name="hyena_long_conv",
    )(u, g, spec_r, spec_i, tw[0], tw[1], *_pair_factors(n1f, True), fst, bias[order][None, :])


def _hidden_body(f_ref, w1_ref, b1_ref, w2_ref, b2_ref, fr_ref, o_ref):
    hi = lax.Precision.HIGHEST
    fr = fr_ref[...]
    h = jnp.sin(fr * (jnp.dot(f_ref[...], w1_ref[...], precision=hi, preferred_element_type=F32) + b1_ref[...]))
    o_ref[...] = jnp.sin(fr * (jnp.dot(h, w2_ref[...], precision=hi, preferred_element_type=F32) + b2_ref[...]))


def _filter_features(seq):
    bands = np.linspace(1e-4, HY_BANDS - 1, HY_BANDS)[None, :]

    def feats(j):
        wpos = (2.0 * math.pi / seq) * j
        return np.concatenate([j / (seq - 1.0), np.cos(bands * wpos), -np.sin(bands * wpos)], axis=-1)

    j = np.arange(seq, dtype=np.float64)[:, None]
    return jnp.asarray(np.concatenate([feats(j), feats(seq - j)], axis=1), F32)


def _blockdiag2(w):
    z = jnp.zeros_like(w)
    return jnp.concatenate([jnp.concatenate([w, z], axis=1), jnp.concatenate([z, w], axis=1)], axis=0)


def _filter_hidden(feats, w1, b1, w2, b2, freq):
    seq = feats.shape[0]
    two = lambda v: jnp.concatenate([v, v])[None, :]
    tr = 512
    full = lambda a: pl.BlockSpec(a.shape, lambda i: (0, 0))
    args = (_blockdiag2(w1), two(b1), _blockdiag2(w2), two(b2), two(freq))
    return pl.pallas_call(
        _hidden_body,
        grid=(seq // tr,),
        in_specs=[pl.BlockSpec((tr, 2 * HY_EMB), lambda i: (i, 0))] + [full(a) for a in args],
        out_specs=pl.BlockSpec((tr, 2 * HY_HIDDEN), lambda i: (i, 0)),
        out_shape=jax.ShapeDtypeStruct((seq, 2 * HY_HIDDEN), F32),
        compiler_params=_params(("parallel",)),
        name="hyena_filter_hidden",
    )(feats, *args)


FILTER_ROWS = 512


def _split_bf16(a):
    hi = a.astype(BF16)
    return hi, (a - hi.astype(F32)).astype(BF16)


def _hfilter_body(h_ref, w3_ref, d_ref, twr_ref, twi_ref, cr_ref, ci_ref, fst_ref, or_ref, oi_ref, k_scr, q_scr,
                  *, seq):
    n1f = 2 * seq // DFT_N
    dec = jnp.abs(d_ref[...])
    w_hi, w_lo = _split_bf16(w3_ref[...])
    norm = jnp.zeros((1, LANES), F32)
    for r0 in range(0, seq, FILTER_ROWS):
        j = (lax.broadcasted_iota(jnp.int32, (FILTER_ROWS, 1), 0) + r0).astype(F32)
        h_hi, h_lo = _split_bf16(h_ref[r0:r0 + FILTER_ROWS, :])
        k2 = (jnp.dot(h_hi, w_hi, preferred_element_type=F32) + jnp.dot(h_hi, w_lo, preferred_element_type=F32)
              + jnp.dot(h_lo, w_hi, preferred_element_type=F32))
        kf = k2[:, :LANES] * jnp.exp(-(j / (seq - 1.0)) * dec[:, :LANES])
        kb = k2[:, LANES:] * jnp.exp(-((seq - j) / (seq - 1.0)) * dec[:, LANES:])
        kb = jnp.where(j > 0.0, kb, 0.0)
        norm = norm + jnp.sum(jnp.abs(kf), axis=0, keepdims=True) + jnp.sum(jnp.abs(kb), axis=0, keepdims=True)
        k_scr[r0:r0 + FILTER_ROWS, :] = kf
        k_scr[seq + r0:seq + r0 + FILTER_ROWS, :] = kb
    scale = (1.0 / (2 * seq)) / norm
    for r0 in range(0, 2 * seq, FILTER_ROWS):
        k_scr[r0:r0 + FILTER_ROWS, :] = k_scr[r0:r0 + FILTER_ROWS, :] * scale
    fst = fst_ref[...]

    def fwd(i, carry):
        r = pl.multiple_of(i * SUBLANES, SUBLANES)
        xs = [(k_scr[pl.ds(t1 * DFT_N + r, SUBLANES), :], None) for t1 in range(n1f)]
        _store_rows(q_scr, _forward_rows(xs, n1f), r, xs[0][0])
        return carry

    lax.fori_loop(0, DFT_N // SUBLANES, fwd, 0)

    def mid(i, carry):
        pairs = [_pair_spectra(fst, q_scr, twr_ref, twi_ref, cr_ref, ci_ref, i * (FILTER_MID_BLOCKS // 2) + k,
                               n1f // 2) for k in range(FILTER_MID_BLOCKS // 2)]
        for pair in pairs:
            for r, (xr, xi) in pair:
                or_ref[r, :] = xr.astype(or_ref.dtype)
                oi_ref[r, :] = xi.astype(oi_ref.dtype)
        return carry

    lax.fori_loop(0, n1f // FILTER_MID_BLOCKS, mid, 0)


def _by_block(a):
    nc = D_GROUP // LANES
    lead = a.shape[:-1]
    a = a.reshape(lead + (HY_ORDER, 2, nc, LANES))
    return jnp.swapaxes(a, -3, -2).reshape(lead + (HY_ORDER * nc * 2 * LANES,))


def _hfilter(hid, w3, decay, tw, fst, seq):
    n1f = 2 * seq // DFT_N
    nc = D_GROUP // LANES
    direction = jnp.arange(w3.shape[1]) // D_GROUP % 2
    w3 = jnp.concatenate([jnp.where(direction == 0, w3, 0.0), jnp.where(direction == 1, w3, 0.0)], axis=0)
    w3 = _by_block(w3)
    dec = _by_block(decay)[None, :]
    once = pl.Buffered(1)
    pair = lambda o, c: (0, nc * o + c)
    out_spec = pl.BlockSpec((None, n1f * DFT_N, LANES), lambda o, c: (o, 0, c))
    out_shape = jax.ShapeDtypeStruct((HY_ORDER, n1f * DFT_N, D_GROUP), BF16)
    return pl.pallas_call(
        functools.partial(_hfilter_body, seq=seq),
        grid=(HY_ORDER, nc),
        in_specs=[
            pl.BlockSpec((seq, 2 * HY_HIDDEN), lambda o, c: (0, 0)),
            pl.BlockSpec((2 * HY_HIDDEN, 2 * LANES), pair),
            pl.BlockSpec((1, 2 * LANES), pair),
            pl.BlockSpec((n1f * DFT_N, LANES), lambda o, c: (0, 0), pipeline_mode=once),
            pl.BlockSpec((n1f * DFT_N, LANES), lambda o, c: (0, 0), pipeline_mode=once),
            pl.BlockSpec(memory_space=pltpu.SMEM), pl.BlockSpec(memory_space=pltpu.SMEM),
            pl.BlockSpec((2, DFT_N, 2 * DFT_N), lambda o, c: (0, 0, 0)),
        ],
        out_specs=[out_spec, out_spec],
        out_shape=[out_shape, out_shape],
        scratch_shapes=[pltpu.VMEM((2 * seq, LANES), F32), pltpu.VMEM((n1f * DFT_N, 2 * LANES), F32)],
        compiler_params=_params(("arbitrary", "arbitrary")),
        name="hyena_filter_spectrum",
    )(hid, w3, dec, tw[0], tw[1], *_pair_factors(n1f, False), fst)


def _trunk(x, p, fst):
    nb, seq, _ = x.shape
    x2d = x.reshape(nb * seq, D_MODEL)
    tw = _twiddle_table(2 * seq // DFT_N)
    feats = _filter_features(seq)
    vcol = 0
    g1col = D_GROUP // LANES
    g2col = 2 * D_GROUP // LANES
    for l in range(DEPTH):
        qkv, out_a, out_c, hz = _inmix(x2d, p["mix_norm_g"][l][None, :], p["w_in"], p["pool_w"],
                                       p["pool_scale"][l][None, :], p["sconv_w"][l], p["hy_short_w"][l], l, seq)
        hid = _filter_hidden(feats, p["hy_w1"][l], p["hy_b1"][l], p["hy_w2"][l], p["hy_b2"][l], p["hy_freq"][l])
        spec_r, spec_i = _hfilter(hid, p["hy_w3"][l], p["hy_decay"][l], tw, fst, seq)
        bias = p["hy_bias"][l]
        u1 = _hconv(hz, vcol, hz, g1col, spec_r, spec_i, 0, tw, fst, bias, nb, seq, F32)
        out_d = _hconv(u1, 0, hz, g2col, spec_r, spec_i, 1, tw, fst, bias, nb, seq, BF16)
        x2d = _attn_outproj(x2d, qkv, p["attn_sink"][l], out_a, out_c, out_d, p["w_out"], l, seq)
        x2d = _ffn(x2d, p["ffn_norm_g"][l][None, :], p["w_gate_up"], p["w_down"],
                   p["final_norm_g"][None, :], l, final_norm=(l == DEPTH - 1))
    return x2d.reshape(nb, seq, D_MODEL)


def kernel(x_prompt, x_sample, mix_norm_g, w_in, pool_w, pool_scale, attn_sink, sconv_w, hy_short_w, hy_w1, hy_b1, hy_w2, hy_b2, hy_w3, hy_freq, hy_decay, hy_bias, w_out, ffn_norm_g, w_gate_up, w_down, final_norm_g):
    p = dict(
        mix_norm_g=mix_norm_g, w_in=w_in.astype(BF16), pool_w=pool_w.astype(BF16), pool_scale=pool_scale,
        attn_sink=attn_sink, sconv_w=sconv_w, hy_short_w=hy_short_w, hy_w1=hy_w1, hy_b1=hy_b1, hy_w2=hy_w2,
        hy_b2=hy_b2, hy_w3=hy_w3, hy_freq=hy_freq, hy_decay=hy_decay, hy_bias=hy_bias,
        w_out=w_out.astype(BF16), ffn_norm_g=ffn_norm_g, w_gate_up=w_gate_up.astype(BF16),
        w_down=w_down.astype(BF16), final_norm_g=final_norm_g,
    )
    fst = _dft_stack().astype(BF16)
    return (_trunk(x_prompt, p, fst), _trunk(x_sample, p, fst))
```

```python
import functools
import math

import numpy as np
import jax
import jax.numpy as jnp
from jax import lax
from jax.experimental import pallas as pl
from jax.experimental.pallas import tpu as pltpu

F32 = jnp.float32
BF16 = jnp.bfloat16

D_MODEL = 2048
DEPTH = 2
D_GROUP = 512
POOL_WINDOWS = (2, 4, 8, 16)
POOL_CH = 128
HEAD_DIM = 64
N_Q_HEADS = 8
N_KV_HEADS = 2
GQA_GROUP = 4
BLOCK = 128
ROPE_DIM = 16
ROPE_THETA = 500000.0
HY_ORDER = 2
HY_EMB = 33
HY_BANDS = 16
HY_HIDDEN = 64
D_FF = 5632
NORM_EPS = 1e-6
OFF_Q = 512
OFF_K = 1024
OFF_V = 1152
OFF_C = 1280
OFF_D = 2816
D_IN = 4352

LANES = 128
SUBLANES = 8
DFT_N = 256
VMEM_LIMIT = 56 * 2 ** 20


def _params(sem, vmem=VMEM_LIMIT):
    return pltpu.CompilerParams(dimension_semantics=sem, vmem_limit_bytes=vmem)


def _rms(x, g):
    ms = jnp.mean(x * x, axis=-1, keepdims=True)
    return x * lax.rsqrt(ms + NORM_EPS) * g


HALO = SUBLANES
D_QKV = OFF_C - OFF_Q


def _conv3_ext(z, w_ref):
    n = z.shape[0]
    return (pltpu.roll(z, 1, axis=0) * w_ref[0:1, :] + z * w_ref[1:2, :]
            + pltpu.roll(z, n - 1, axis=0) * w_ref[2:3, :])


def _inmix_body(x_ref, xp_ref, xn_ref, g_ref, w_ref, pw_ref, ps_ref, cw_ref, hw_ref,
                qkv_ref, a_ref, c_ref, hz_ref, *, seq):
    tm = x_ref.shape[0]
    ext = tm + 2 * HALO
    core = slice(HALO, HALO + tm)
    g = g_ref[...]
    pos0 = (pl.program_id(0) * tm) % seq
    before = jnp.where(pos0 > 0, _rms(xp_ref[...], g), 0.0)
    after = jnp.where(pos0 + tm < seq, _rms(xn_ref[...], g), 0.0)
    hn = jnp.concatenate([before, _rms(x_ref[...], g), after], axis=0).astype(BF16)
    row = lax.broadcasted_iota(jnp.int32, (ext, 1), 0)

    def project(lo, hi):
        return jnp.dot(hn, w_ref[:, lo:hi], preferred_element_type=F32)

    qkv_ref[...] = project(OFF_Q, OFF_C)[core]

    za = project(0, OFF_Q)
    pos = pos0 + row - HALO
    for k, win in enumerate(POOL_WINDOWS):
        cols = slice(k * POOL_CH, (k + 1) * POOL_CH)
        p = za[:, cols]
        s = p + pltpu.roll(p, 1, axis=0)
        step = 1
        while 2 * step < win:
            s = pltpu.roll(s, step, axis=0) + pltpu.roll(s, ext - step, axis=0)
            step *= 2
        half = win // 2
        cnt = jnp.maximum(jnp.minimum(pos + half, seq) - jnp.maximum(pos - half, 0), 1).astype(F32)
        d = (s / cnt - p)[core]
        out = jnp.dot(d.astype(BF16), pw_ref[k], preferred_element_type=F32) * ps_ref[:, cols]
        a_ref[:, cols] = out.astype(a_ref.dtype)

    zc = project(OFF_C, OFF_D)
    ch, cb, cc = zc[:, :D_GROUP], zc[:, D_GROUP:2 * D_GROUP], zc[:, 2 * D_GROUP:]
    c_ref[...] = (cb * _conv3_ext(cc * ch, cw_ref))[core].astype(c_ref.dtype)

    hz_ref[...] = _conv3_ext(project(OFF_D, D_IN), hw_ref)[core]


def _inmix(x2d, g, w, pool_w, pool_scale, sconv_w, hy_short_w, layer, seq):
    T = x2d.shape[0]
    tm = 512
    per = tm // HALO
    last = T // HALO - 1
    full = lambda a: pl.BlockSpec(a.shape, lambda i: (0, 0))
    row_tile = lambda width: pl.BlockSpec((tm, width), lambda i: (i, 0))
    return pl.pallas_call(
        functools.partial(_inmix_body, seq=seq),
        grid=(T // tm,),
        in_specs=[
            row_tile(D_MODEL),
            pl.BlockSpec((HALO, D_MODEL), lambda i: (jnp.maximum(i * per - 1, 0), 0)),
            pl.BlockSpec((HALO, D_MODEL), lambda i: (jnp.minimum((i + 1) * per, last), 0)),
            full(g),
            pl.BlockSpec((None, D_MODEL, D_IN), lambda i: (layer, 0, 0), pipeline_mode=pl.Buffered(1)),
            pl.BlockSpec((None, len(POOL_WINDOWS), POOL_CH, POOL_CH), lambda i: (layer, 0, 0, 0)),
            full(pool_scale), full(sconv_w), full(hy_short_w),
        ],
        out_specs=[row_tile(D_QKV), row_tile(D_GROUP), row_tile(D_GROUP), row_tile(3 * D_GROUP)],
        out_shape=[jax.ShapeDtypeStruct((T, D_QKV), F32), jax.ShapeDtypeStruct((T, D_GROUP), BF16),
                   jax.ShapeDtypeStruct((T, D_GROUP), BF16), jax.ShapeDtypeStruct((T, 3 * D_GROUP), F32)],
        compiler_params=_params(("parallel",)),
        name="inproj_local_mixers",
    )(x2d, x2d, x2d, g, w, pool_w, pool_scale, sconv_w, hy_short_w)


def _ffn_body(x_ref, g_ref, wg_ref, wu_ref, wd_ref, gf_ref, o_ref, h_ref, *, final_norm):
    j = pl.program_id(1)

    @pl.when(j == 0)
    def _():
        x = x_ref[...]
        h_ref[...] = _rms(x, g_ref[...]).astype(BF16)
        o_ref[...] = x

    h = h_ref[...]
    gate = jnp.dot(h, wg_ref[...], preferred_element_type=F32)
    up = jnp.dot(h, wu_ref[...], preferred_element_type=F32)
    act = (gate * jax.nn.sigmoid(gate) * up).astype(BF16)
    o_ref[...] += jnp.dot(act, wd_ref[...], preferred_element_type=F32)

    if final_norm:
        @pl.when(j == pl.num_programs(1) - 1)
        def _():
            o_ref[...] = _rms(o_ref[...], gf_ref[...])


def _ffn(x2d, g, w_gate_up, w_down, gf, layer, final_norm):
    T = x2d.shape[0]
    tm, tf = 1024, 512
    nf = D_FF // tf
    return pl.pallas_call(
        functools.partial(_ffn_body, final_norm=final_norm),
        grid=(T // tm, nf),
        in_specs=[
            pl.BlockSpec((tm, D_MODEL), lambda i, j: (i, 0)),
            pl.BlockSpec((1, D_MODEL), lambda i, j: (0, 0)),
            pl.BlockSpec((None, D_MODEL, tf), lambda i, j: (layer, 0, j)),
            pl.BlockSpec((None, D_MODEL, tf), lambda i, j: (layer, 0, j + nf)),
            pl.BlockSpec((None, tf, D_MODEL), lambda i, j: (layer, j, 0)),
            pl.BlockSpec((1, D_MODEL), lambda i, j: (0, 0)),
        ],
        out_specs=pl.BlockSpec((tm, D_MODEL), lambda i, j: (i, 0)),
        out_shape=jax.ShapeDtypeStruct((T, D_MODEL), F32),
        scratch_shapes=[pltpu.VMEM((tm, D_MODEL), BF16)],
        compiler_params=_params(("parallel", "arbitrary")),
        name="ffn",
    )(x2d, g, w_gate_up, w_gate_up, w_down, gf)


def _rope_tables(seq):
    inv_freq = ROPE_THETA ** (-np.arange(0, ROPE_DIM, 2, dtype=np.float64) / ROPE_DIM)
    ang = np.arange(seq, dtype=np.float64)[:, None] * inv_freq[None, :]
    half = ROPE_DIM // 2
    cos = np.ones((seq, HEAD_DIM))
    s_lo = np.zeros((seq, HEAD_DIM))
    s_hi = np.zeros((seq, HEAD_DIM))
    cos[:, :half] = np.cos(ang)
    cos[:, half:ROPE_DIM] = np.cos(ang)
    s_lo[:, half:ROPE_DIM] = np.sin(ang)
    s_hi[:, :half] = -np.sin(ang)
    rep = LANES // HEAD_DIM
    return tuple(jnp.asarray(np.tile(a, (1, rep)), F32) for a in (cos, s_lo, s_hi))


LOG2E = math.log2(math.e)
ATTN_QBLOCKS = 4


def _rope(x, cos, s_lo, s_hi):
    half = ROPE_DIM // 2
    return x * cos + pltpu.roll(x, half, axis=1) * s_lo + pltpu.roll(x, LANES - half, axis=1) * s_hi


def _attn_tile(q_ref, kp_ref, kc_ref, kn_ref, vp_ref, vc_ref, vn_ref, cq_ref, cp_ref, cn_ref, sink_ref, n, nstep):
    two = 2 * BLOCK
    blk_rows = lambda i: slice(i * BLOCK, (i + 1) * BLOCK)
    tq = tuple(r[...] for r in cq_ref)

    q = q_ref[...]
    qscale = HEAD_DIM ** -0.5 * LOG2E
    qg = [(_rope(q[:, g * LANES:(g + 1) * LANES], *tq) * qscale).astype(BF16) for g in range(D_GROUP // LANES)]

    lane = lax.broadcasted_iota(jnp.int32, (BLOCK, LANES), 1)
    first = lane < HEAD_DIM
    half_ones = jnp.where(first, 1.0, 0.0)
    ones_bd = jnp.concatenate([half_ones, 1.0 - half_ones], axis=0)

    def per_head(x):
        xr = pltpu.roll(x, HEAD_DIM, axis=1)
        return [jnp.concatenate([jnp.where(first, x, 0.0), jnp.where(first, 0.0, xr)], axis=0),
                jnp.concatenate([jnp.where(first, xr, 0.0), jnp.where(first, 0.0, x)], axis=0)]

    kc = _rope(kc_ref[...], *tq)
    vc = vc_ref[...]
    kblocks = ([_rope(kp_ref[...], *(r[...] for r in cp_ref))] + [kc[blk_rows(i)] for i in range(ATTN_QBLOCKS)]
               + [_rope(kn_ref[...], *(r[...] for r in cn_ref))])
    vblocks = [vp_ref[...]] + [vc[blk_rows(i)] for i in range(ATTN_QBLOCKS)] + [vn_ref[...]]
    kcat = [[a.astype(BF16) for a in per_head(k)] for k in kblocks]
    vcat = [[jnp.concatenate([a, ones_bd], axis=1).astype(BF16) for a in per_head(v)] for v in vblocks]

    r = lax.broadcasted_iota(jnp.int32, (two, two), 0) % BLOCK
    c = lax.broadcasted_iota(jnp.int32, (two, two), 1)
    left = c < BLOCK
    c = c % BLOCK
    below, above = c >= r, c <= r
    first2 = lax.broadcasted_iota(jnp.int32, (two, LANES), 1) < HEAD_DIM

    tiles = []
    for i in range(ATTN_QBLOCKS):
        lo_mask = (below & (n > 0)) if i == 0 else below
        hi_mask = (above & (n < nstep - 1)) if i == ATTN_QBLOCKS - 1 else above
        masks = [lo_mask, None, hi_mask]
        outs = []
        for kv in range(N_KV_HEADS):
            qs = jnp.concatenate([qg[2 * kv][blk_rows(i)], qg[2 * kv + 1][blk_rows(i)]], axis=0)
            s = []
            for j in range(3):
                sb = lax.dot_general(qs, kcat[i + j][kv], (((1,), (1,)), ((), ())), preferred_element_type=F32)
                if masks[j] is not None:
                    sb = jnp.where(masks[j], sb, -jnp.inf)
                s.append(sb)
            sink = sink_ref[kv]
            top = jnp.maximum(jnp.maximum(s[0], s[1]), s[2])
            m = jnp.where(left, jnp.max(top[:, :BLOCK], axis=-1, keepdims=True),
                          jnp.max(top[:, BLOCK:], axis=-1, keepdims=True))
            m = jnp.maximum(m, sink)
            acc = sum(jnp.dot(jnp.exp2(s[j] - m).astype(BF16), vcat[i + j][kv], preferred_element_type=F32)
                      for j in range(3))
            m_out = jnp.where(first2, m[:, :LANES], m[:, LANES:])
            sink_out = jnp.where(first2, sink[:, :LANES], sink[:, LANES:])
            o = acc[:, :LANES] / (acc[:, LANES:] + jnp.exp2(sink_out - m_out))
            outs += [o[:BLOCK], o[BLOCK:]]
        tiles.append(jnp.concatenate(outs, axis=1).astype(BF16))
    return jnp.concatenate(tiles, axis=0)


def _attn_outproj_body(x_ref, a_ref, c_ref, d_ref, w_ref, *rest, nstep):
    attn_refs, o_ref = rest[:-1], rest[-1]
    acc = x_ref[...]
    for k, m_ref in ((0, a_ref), (2, c_ref), (3, d_ref)):
        acc = acc + jnp.dot(m_ref[...], w_ref[k * D_GROUP:(k + 1) * D_GROUP, :], preferred_element_type=F32)
    b = _attn_tile(*attn_refs, pl.program_id(0) % nstep, nstep)
    o_ref[...] = acc + jnp.dot(b, w_ref[D_GROUP:2 * D_GROUP, :], preferred_element_type=F32)


def _attn_outproj(x2d, qkv, sink, out_a, out_c, out_d, w, layer, seq):
    T = x2d.shape[0]
    rows = ATTN_QBLOCKS * BLOCK
    nblk = seq // BLOCK
    nstep = seq // rows
    last_blk = T // BLOCK - 1
    tabs = _rope_tables(seq)
    sink_tab = jnp.repeat(jnp.repeat((sink.astype(F32) * LOG2E).reshape(N_KV_HEADS, 2, 2), BLOCK, axis=1),
                          BLOCK, axis=2)
    kcol, vcol = (OFF_K - OFF_Q) // LANES, (OFF_V - OFF_Q) // LANES

    prev = lambda i: jnp.maximum(i * ATTN_QBLOCKS - 1, 0)
    nxt = lambda i: jnp.minimum((i + 1) * ATTN_QBLOCKS, last_blk)

    def own(width, col=0):
        return pl.BlockSpec((rows, width), lambda i: (i, col))

    def edge(col, sel):
        return pl.BlockSpec((BLOCK, LANES), lambda i: (sel(i), col))

    def tab_edge(sel):
        return [pl.BlockSpec((BLOCK, LANES), lambda i: (sel(i) % nblk, 0))] * 3

    return pl.pallas_call(
        functools.partial(_attn_outproj_body, nstep=nstep),
        grid=(T // rows,),
        in_specs=[
            own(D_MODEL), own(D_GROUP), own(D_GROUP), own(D_GROUP),
            pl.BlockSpec((None, D_MODEL, D_MODEL), lambda i: (layer, 0, 0), pipeline_mode=pl.Buffered(1)),
            own(D_GROUP),
            edge(kcol, prev), own(LANES, kcol), edge(kcol, nxt),
            edge(vcol, prev), own(LANES, vcol), edge(vcol, nxt),
            [pl.BlockSpec((rows, LANES), lambda i: (i % nstep, 0))] * 3, tab_edge(prev), tab_edge(nxt),
            pl.BlockSpec((N_KV_HEADS, 2 * BLOCK, 2 * BLOCK), lambda i: (0, 0, 0)),
        ],
        out_specs=own(D_MODEL),
        out_shape=jax.ShapeDtypeStruct((T, D_MODEL), F32),
        compiler_params=_params(("parallel",)),
        name="attention_outproj",
    )(x2d, out_a, out_c, out_d, w, qkv, qkv, qkv, qkv, qkv, qkv, qkv, list(tabs), list(tabs), list(tabs), sink_tab)


def _is(x, v):
    return abs(x - v) < 1e-12


def _scale(a, s):
    if a is None or _is(s, 0.0):
        return None
    if _is(s, 1.0):
        return a
    if _is(s, -1.0):
        return -a
    return a * s


def _cmul_const(a, w):
    ar, ai = a
    return (_psub(_scale(ar, w.real), _scale(ai, w.imag)), _padd(_scale(ar, w.imag), _scale(ai, w.real)))


def _padd(x, y):
    if x is None:
        return y
    if y is None:
        return x
    return x + y


def _psub(x, y):
    if y is None:
        return x
    if x is None:
        return -y
    return x - y


def _fft_blocks(xs, sign, leaves_done=False):
    n = len(xs)
    if n == 1 or (n == 2 and leaves_done):
        return list(xs)
    ev = _fft_blocks(xs[0::2], sign, leaves_done)
    od = _fft_blocks(xs[1::2], sign, leaves_done)
    out = [None] * n
    for k in range(n // 2):
        t = _cmul_const(od[k], np.exp(sign * 2j * math.pi * k / n))
        out[k] = (_padd(ev[k][0], t[0]), _padd(ev[k][1], t[1]))
        out[k + n // 2] = (_psub(ev[k][0], t[0]), _psub(ev[k][1], t[1]))
    return out


def _dense(a, like):
    return jnp.zeros_like(like) if a is None else a


def _dft_stack():
    k = np.arange(DFT_N, dtype=np.float64)
    ang = 2.0 * math.pi * np.outer(k, k) / DFT_N
    c, s = np.cos(ang), np.sin(ang)
    return jnp.asarray(np.stack([np.concatenate([c, -s], axis=1), np.concatenate([c, s], axis=1)]), F32)


def _twiddle_table(n1f):
    f1 = np.arange(n1f, dtype=np.float64)[:, None]
    t2 = np.arange(DFT_N, dtype=np.float64)[None, :]
    ang = (-2.0 * math.pi / (n1f * DFT_N)) * f1 * t2
    rep = lambda a: jnp.asarray(np.repeat(a.reshape(-1, 1), LANES, axis=1), F32)
    return rep(np.cos(ang)), rep(np.sin(ang))


def _block_dft(f, qr, qi):
    qr, qi = qr.astype(BF16), qi.astype(BF16)
    rhs = jnp.concatenate([jnp.concatenate([qr, qi], axis=1), jnp.concatenate([-qi, qr], axis=1)], axis=0)
    d = jnp.dot(f, rhs, preferred_element_type=F32)
    return d[:, 0:LANES], d[:, LANES:]


def _forward_rows(xs, n1f):
    half = n1f // 2
    out = [None] * n1f
    if len(xs) == n1f:
        out[:half] = _fft_blocks(xs[0::2], -1.0)
        out[half:] = _fft_blocks(xs[1::2], -1.0)
        return out
    branches = (xs, [_cmul_const(x, np.exp(-2j * math.pi * t / n1f)) for t, x in enumerate(xs)])
    for b, xb in enumerate(branches):
        out[b:half:2] = _fft_blocks(xb[0::2], -1.0)
        out[half + b::2] = _fft_blocks(xb[1::2], -1.0)
    return out


def _pair_factors(n1f, pruned):
    f = np.arange(n1f // 2, dtype=np.float64)
    ang = -2.0 * math.pi * ((f // 2) / (n1f // 2) if pruned else f / n1f)
    return jnp.asarray(np.cos(ang), F32), jnp.asarray(np.sin(ang), F32)


MID_BLOCKS = 8
FILTER_MID_BLOCKS = 8


def _block_rows(f):
    start = f * DFT_N
    return pl.ds(start if isinstance(start, int) else pl.multiple_of(start, DFT_N), DFT_N)


def _store_rows(q_scr, blocks, r, like):
    for f, (pr, pi) in enumerate(blocks):
        rows = pl.ds(f * DFT_N + r, SUBLANES)
        q_scr[rows, 0:LANES] = _dense(pr, like)
        q_scr[rows, LANES:] = _dense(pi, like)


def _pair_spectra(fst, q_scr, twr_ref, twi_ref, cr_ref, ci_ref, f, half):
    lo, hi = _block_rows(f), _block_rows(f + half)
    er, ei = q_scr[lo, 0:LANES], q_scr[lo, LANES:]
    orr, oi = q_scr[hi, 0:LANES], q_scr[hi, LANES:]
    cr, ci = cr_ref[f], ci_ref[f]
    wr, wi = orr * cr - oi * ci, orr * ci + oi * cr
    out = []
    for rows, pr, pi in ((lo, er + wr, ei + wi), (hi, er - wr, ei - wi)):
        tr, ti = twr_ref[rows, :], twi_ref[rows, :]
        out.append((rows, _block_dft(fst[0], pr * tr - pi * ti, pr * ti + pi * tr)))
    return out


def _hconv_body(u_ref, g_ref, kr_ref, ki_ref, twr_ref, twi_ref, cr_ref, ci_ref, fst_ref, b_ref, o_ref, q_scr,
                *, seq):
    n1 = seq // DFT_N
    n1f = 2 * n1
    fst = fst_ref[...]

    def fwd(i, carry):
        r = pl.multiple_of(i * SUBLANES, SUBLANES)
        xs = [(u_ref[pl.ds(t * DFT_N + r, SUBLANES), :], u_ref[pl.ds(seq + t * DFT_N + r, SUBLANES), :])
              for t in range(n1)]
        _store_rows(q_scr, _forward_rows(xs, n1f), r, xs[0][0])
        return carry

    lax.fori_loop(0, DFT_N // SUBLANES, fwd, 0)

    def mid(i, carry):
        pairs = [_pair_spectra(fst, q_scr, twr_ref, twi_ref, cr_ref, ci_ref, i * (MID_BLOCKS // 2) + k, n1)
                 for k in range(MID_BLOCKS // 2)]
        s = []
        for pair in pairs:
            sp = []
            for r, (xr, xi) in pair:
                kr, ki = kr_ref[r, :].astype(F32), ki_ref[r, :].astype(F32)
                rr, ri = _block_dft(fst[1], xr * kr - xi * ki, xr * ki + xi * kr)
                tr, ti = twr_ref[r, :], twi_ref[r, :]
                sp.append((rr * tr + ri * ti, ri * tr - rr * ti))
            (lo, _), (hi, _) = pair
            s.append((lo, sp[0][0] + sp[1][0], sp[0][1] + sp[1][1]))
            s.append((hi, sp[0][0] - sp[1][0], sp[0][1] - sp[1][1]))
        for r, sr_, si_ in s:
            q_scr[r, 0:LANES] = sr_
            q_scr[r, LANES:] = si_
        return carry

    lax.fori_loop(0, n1f // MID_BLOCKS, mid, 0)

    bias = b_ref[...]

    def inv(i, carry):
        r = pl.multiple_of(i * SUBLANES, SUBLANES)
        ss = [(q_scr[pl.ds(f * DFT_N + r, SUBLANES), 0:LANES], q_scr[pl.ds(f * DFT_N + r, SUBLANES), LANES:])
              for f in range(n1f)]
        ev = _fft_blocks(ss[0::2], 1.0, leaves_done=True)
        od = _fft_blocks(ss[1::2], 1.0, leaves_done=True)
        for t in range(n1):
            o = _cmul_const(od[t], np.exp(2j * math.pi * t / n1f))
            for b, y in enumerate((ev[t][0] + o[0], ev[t][1] + o[1])):
                rows = pl.ds(b * seq + t * DFT_N + r, SUBLANES)
                o_ref[rows, :] = (g_ref[rows, :] * (y + bias * u_ref[rows, :])).astype(o_ref.dtype)
        return carry

    lax.fori_loop(0, DFT_N // SUBLANES, inv, 0)


def _hconv(u, ucol, g, gcol, spec_r, spec_i, order, tw, fst, bias, nb, seq, out_dtype):
    n1f = 2 * seq // DFT_N
    nc = D_GROUP // LANES
    once = pl.Buffered(1)
    return pl.pallas_call(
        functools.partial(_hconv_body, seq=seq),
        grid=(nc, nb // 2),
        in_specs=[
            pl.BlockSpec((2 * seq, LANES), lambda c, p: (p, ucol + c)),
            pl.BlockSpec((2 * seq, LANES), lambda c, p: (p, gcol + c)),
            pl.BlockSpec((None, n1f * DFT_N, LANES), lambda c, p: (order, 0, c)),
            pl.BlockSpec((None, n1f * DFT_N, LANES), lambda c, p: (order, 0, c)),
            pl.BlockSpec((n1f * DFT_N, LANES), lambda c, p: (0, 0), pipeline_mode=once),
            pl.BlockSpec((n1f * DFT_N, LANES), lambda c, p: (0, 0), pipeline_mode=once),
            pl.BlockSpec(memory_space=pltpu.SMEM), pl.BlockSpec(memory_space=pltpu.SMEM),
            pl.BlockSpec((2, DFT_N, 2 * DFT_N), lambda c, p: (0, 0, 0)),
            pl.BlockSpec((1, LANES), lambda c, p: (0, c)),
        ],
        out_specs=pl.BlockSpec((2 * seq, LANES), lambda c, p: (p, c)),
        out_shape=jax.ShapeDtypeStruct((nb * seq, D_GROUP), out_dtype),
        scratch_shapes=[pltpu.VMEM((n1f * DFT_N, 2 * LANES), F32)],
        compiler_params=_params(("arbitrary", "arbitrary")),
        name="hyena_long_conv",
    )(u, g, spec_r, spec_i, tw[0], tw[1], *_pair_factors(n1f, True), fst, bias[order][None, :])


def _hidden_body(f_ref, w1_ref, b1_ref, w2_ref, b2_ref, fr_ref, o_ref):
    hi = lax.Precision.HIGHEST
    fr = fr_ref[...]
    h = jnp.sin(fr * (jnp.dot(f_ref[...], w1_ref[...], precision=hi, preferred_element_type=F32) + b1_ref[...]))
    o_ref[...] = jnp.sin(fr * (jnp.dot(h, w2_ref[...], precision=hi, preferred_element_type=F32) + b2_ref[...]))


def _filter_features(seq):
    bands = np.linspace(1e-4, HY_BANDS - 1, HY_BANDS)[None, :]

    def feats(j):
        wpos = (2.0 * math.pi / seq) * j
        return np.concatenate([j / (seq - 1.0), np.cos(bands * wpos), -np.sin(bands * wpos)], axis=-1)

    j = np.arange(seq, dtype=np.float64)[:, None]
    return jnp.asarray(np.concatenate([feats(j), feats(seq - j)], axis=1), F32)


def _blockdiag2(w):
    z = jnp.zeros_like(w)
    return jnp.concatenate([jnp.concatenate([w, z], axis=1), jnp.concatenate([z, w], axis=1)], axis=0)


def _filter_hidden(feats, w1, b1, w2, b2, freq):
    seq = feats.shape[0]
    two = lambda v: jnp.concatenate([v, v])[None, :]
    tr = 512
    full = lambda a: pl.BlockSpec(a.shape, lambda i: (0, 0))
    args = (_blockdiag2(w1), two(b1), _blockdiag2(w2), two(b2), two(freq))
    return pl.pallas_call(
        _hidden_body,
        grid=(seq // tr,),
        in_specs=[pl.BlockSpec((tr, 2 * HY_EMB), lambda i: (i, 0))] + [full(a) for a in args],
        out_specs=pl.BlockSpec((tr, 2 * HY_HIDDEN), lambda i: (i, 0)),
        out_shape=jax.ShapeDtypeStruct((seq, 2 * HY_HIDDEN), F32),
        compiler_params=_params(("parallel",)),
        name="hyena_filter_hidden",
    )(feats, *args)


FILTER_ROWS = 512


def _split_bf16(a):
    hi = a.astype(BF16)
    return hi, (a - hi.astype(F32)).astype(BF16)


def _hfilter_body(h_ref, w3_ref, d_ref, twr_ref, twi_ref, cr_ref, ci_ref, fst_ref, or_ref, oi_ref, k_scr, q_scr,
                  *, seq):
    n1f = 2 * seq // DFT_N
    dec = jnp.abs(d_ref[...])
    w_hi, w_lo = _split_bf16(w3_ref[...])
    norm = jnp.zeros((1, LANES), F32)
    for r0 in range(0, seq, FILTER_ROWS):
        j = (lax.broadcasted_iota(jnp.int32, (FILTER_ROWS, 1), 0) + r0).astype(F32)
        h_hi, h_lo = _split_bf16(h_ref[r0:r0 + FILTER_ROWS, :])
        k2 = (jnp.dot(h_hi, w_hi, preferred_element_type=F32) + jnp.dot(h_hi, w_lo, preferred_element_type=F32)
              + jnp.dot(h_lo, w_hi, preferred_element_type=F32))
        kf = k2[:, :LANES] * jnp.exp(-(j / (seq - 1.0)) * dec[:, :LANES])
        kb = k2[:, LANES:] * jnp.exp(-((seq - j) / (seq - 1.0)) * dec[:, LANES:])
        kb = jnp.where(j > 0.0, kb, 0.0)
        norm = norm + jnp.sum(jnp.abs(kf), axis=0, keepdims=True) + jnp.sum(jnp.abs(kb), axis=0, keepdims=True)
        k_scr[r0:r0 + FILTER_ROWS, :] = kf
        k_scr[seq + r0:seq + r0 + FILTER_ROWS, :] = kb
    scale = (1.0 / (2 * seq)) / norm
    for r0 in range(0, 2 * seq, FILTER_ROWS):
        k_scr[r0:r0 + FILTER_ROWS, :] = k_scr[r0:r0 + FILTER_ROWS, :] * scale
    fst = fst_ref[...]

    def fwd(i, carry):
        r = pl.multiple_of(i * SUBLANES, SUBLANES)
        xs = [(k_scr[pl.ds(t1 * DFT_N + r, SUBLANES), :], None) for t1 in range(n1f)]
        _store_rows(q_scr, _forward_rows(xs, n1f), r, xs[0][0])
        return carry

    lax.fori_loop(0, DFT_N // SUBLANES, fwd, 0)

    def mid(i, carry):
        pairs = [_pair_spectra(fst, q_scr, twr_ref, twi_ref, cr_ref, ci_ref, i * (FILTER_MID_BLOCKS // 2) + k,
                               n1f // 2) for k in range(FILTER_MID_BLOCKS // 2)]
        for pair in pairs:
            for r, (xr, xi) in pair:
                or_ref[r, :] = xr.astype(or_ref.dtype)
                oi_ref[r, :] = xi.astype(oi_ref.dtype)
        return carry

    lax.fori_loop(0, n1f // FILTER_MID_BLOCKS, mid, 0)


def _by_block(a):
    nc = D_GROUP // LANES
    lead = a.shape[:-1]
    a = a.reshape(lead + (HY_ORDER, 2, nc, LANES))
    return jnp.swapaxes(a, -3, -2).reshape(lead + (HY_ORDER * nc * 2 * LANES,))


def _hfilter(hid, w3, decay, tw, fst, seq):
    n1f = 2 * seq // DFT_N
    nc = D_GROUP // LANES
    direction = jnp.arange(w3.shape[1]) // D_GROUP % 2
    w3 = jnp.concatenate([jnp.where(direction == 0, w3, 0.0), jnp.where(direction == 1, w3, 0.0)], axis=0)
    w3 = _by_block(w3)
    dec = _by_block(decay)[None, :]
    once = pl.Buffered(1)
    pair = lambda o, c: (0, nc * o + c)
    out_spec = pl.BlockSpec((None, n1f * DFT_N, LANES), lambda o, c: (o, 0, c))
    out_shape = jax.ShapeDtypeStruct((HY_ORDER, n1f * DFT_N, D_GROUP), BF16)
    return pl.pallas_call(
        functools.partial(_hfilter_body, seq=seq),
        grid=(HY_ORDER, nc),
        in_specs=[
            pl.BlockSpec((seq, 2 * HY_HIDDEN), lambda o, c: (0, 0)),
            pl.BlockSpec((2 * HY_HIDDEN, 2 * LANES), pair),
            pl.BlockSpec((1, 2 * LANES), pair),
            pl.BlockSpec((n1f * DFT_N, LANES), lambda o, c: (0, 0), pipeline_mode=once),
            pl.BlockSpec((n1f * DFT_N, LANES), lambda o, c: (0, 0), pipeline_mode=once),
            pl.BlockSpec(memory_space=pltpu.SMEM), pl.BlockSpec(memory_space=pltpu.SMEM),
            pl.BlockSpec((2, DFT_N, 2 * DFT_N), lambda o, c: (0, 0, 0)),
        ],
        out_specs=[out_spec, out_spec],
        out_shape=[out_shape, out_shape],
        scratch_shapes=[pltpu.VMEM((2 * seq, LANES), F32), pltpu.VMEM((n1f * DFT_N, 2 * LANES), F32)],
        compiler_params=_params(("arbitrary", "arbitrary")),
        name="hyena_filter_spectrum",
    )(hid, w3, dec, tw[0], tw[1], *_pair_factors(n1f, False), fst)


def _trunk(x, p, fst):
    nb, seq, _ = x.shape
    x2d = x.reshape(nb * seq, D_MODEL)
    tw = _twiddle_table(2 * seq // DFT_N)
    feats = _filter_features(seq)
    vcol = 0
    g1col = D_GROUP // LANES
    g2col = 2 * D_GROUP // LANES
    for l in range(DEPTH):
        qkv, out_a, out_c, hz = _inmix(x2d, p["mix_norm_g"][l][None, :], p["w_in"], p["pool_w"],
                                       p["pool_scale"][l][None, :], p["sconv_w"][l], p["hy_short_w"][l], l, seq)
        hid = _filter_hidden(feats, p["hy_w1"][l], p["hy_b1"][l], p["hy_w2"][l], p["hy_b2"][l], p["hy_freq"][l])
        spec_r, spec_i = _hfilter(hid, p["hy_w3"][l], p["hy_decay"][l], tw, fst, seq)
        bias = p["hy_bias"][l]
        u1 = _hconv(hz, vcol, hz, g1col, spec_r, spec_i, 0, tw, fst, bias, nb, seq, F32)
        out_d = _hconv(u1, 0, hz, g2col, spec_r, spec_i, 1, tw, fst, bias, nb, seq, BF16)
        x2d = _attn_outproj(x2d, qkv, p["attn_sink"][l], out_a, out_c, out_d, p["w_out"], l, seq)
        x2d = _ffn(x2d, p["ffn_norm_g"][l][None, :], p["w_gate_up"], p["w_down"],
                   p["final_norm_g"][None, :], l, final_norm=(l == DEPTH - 1))
    return x2d.reshape(nb, seq, D_MODEL)


def kernel(x_prompt, x_sample, mix_norm_g, w_in, pool_w, pool_scale, attn_sink, sconv_w, hy_short_w, hy_w1, hy_b1, hy_w2, hy_b2, hy_w3, hy_freq, hy_decay, hy_bias, w_out, ffn_norm_g, w_gate_up, w_down, final_norm_g):
    p = dict(
        mix_norm_g=mix_norm_g, w_in=w_in.astype(BF16), pool_w=pool_w.astype(BF16), pool_scale=pool_scale,
        attn_sink=attn_sink, sconv_w=sconv_w, hy_short_w=hy_short_w, hy_w1=hy_w1, hy_b1=hy_b1, hy_w2=hy_w2,
        hy_b2=hy_b2, hy_w3=hy_w3, hy_freq=hy_freq, hy_decay=hy_decay, hy_bias=hy_bias,
        w_out=w_out.astype(BF16), ffn_norm_g=ffn_norm_g, w_gate_up=w_gate_up.astype(BF16),
        w_down=w_down.astype(BF16), final_norm_g=final_norm_g,
    )
    fst = _dft_stack().astype(BF16)
    return (_trunk(x_prompt, p, fst), _trunk(x_sample, p, fst))
```

```python
import functools
import math

import numpy as np
import jax
import jax.numpy as jnp
from jax import lax
from jax.experimental import pallas as pl
from jax.experimental.pallas import tpu as pltpu

F32 = jnp.float32
BF16 = jnp.bfloat16

D_MODEL = 2048
DEPTH = 2
D_GROUP = 512
POOL_WINDOWS = (2, 4, 8, 16)
POOL_CH = 128
HEAD_DIM = 64
N_Q_HEADS = 8
N_KV_HEADS = 2
GQA_GROUP = 4
BLOCK = 128
ROPE_DIM = 16
ROPE_THETA = 500000.0
HY_ORDER = 2
HY_EMB = 33
HY_BANDS = 16
HY_HIDDEN = 64
D_FF = 5632
NORM_EPS = 1e-6
OFF_Q = 512
OFF_K = 1024
OFF_V = 1152
OFF_C = 1280
OFF_D = 2816
D_IN = 4352

LANES = 128
SUBLANES = 8
DFT_N = 256
VMEM_LIMIT = 56 * 2 ** 20
FFN_VMEM_LIMIT = 60 * 2 ** 20


def _params(sem, vmem=VMEM_LIMIT):
    return pltpu.CompilerParams(dimension_semantics=sem, vmem_limit_bytes=vmem)


def _rms(x, g):
    ms = jnp.mean(x * x, axis=-1, keepdims=True)
    return x * lax.rsqrt(ms + NORM_EPS) * g


HALO = SUBLANES
D_QKV = OFF_C - OFF_Q


def _conv3_ext(z, w_ref):
    n = z.shape[0]
    return (pltpu.roll(z, 1, axis=0) * w_ref[0:1, :] + z * w_ref[1:2, :]
            + pltpu.roll(z, n - 1, axis=0) * w_ref[2:3, :])


def _inmix_body(x_ref, xp_ref, xn_ref, g_ref, w_ref, pw_ref, ps_ref, cw_ref, hw_ref,
                qkv_ref, a_ref, c_ref, hz_ref, *, seq):
    tm = x_ref.shape[0]
    ext = tm + 2 * HALO
    core = slice(HALO, HALO + tm)
    g = g_ref[...]
    pos0 = (pl.program_id(0) * tm) % seq
    before = jnp.where(pos0 > 0, _rms(xp_ref[...], g), 0.0)
    after = jnp.where(pos0 + tm < seq, _rms(xn_ref[...], g), 0.0)
    hn = jnp.concatenate([before, _rms(x_ref[...], g), after], axis=0).astype(BF16)
    row = lax.broadcasted_iota(jnp.int32, (ext, 1), 0)

    def project(lo, hi):
        return jnp.dot(hn, w_ref[:, lo:hi], preferred_element_type=F32)

    qkv_ref[...] = project(OFF_Q, OFF_C)[core]

    za = project(0, OFF_Q)
    pos = pos0 + row - HALO
    for k, win in enumerate(POOL_WINDOWS):
        cols = slice(k * POOL_CH, (k + 1) * POOL_CH)
        p = za[:, cols]
        s = p + pltpu.roll(p, 1, axis=0)
        step = 1
        while 2 * step < win:
            s = pltpu.roll(s, step, axis=0) + pltpu.roll(s, ext - step, axis=0)
            step *= 2
        half = win // 2
        cnt = jnp.maximum(jnp.minimum(pos + half, seq) - jnp.maximum(pos - half, 0), 1).astype(F32)
        d = (s / cnt - p)[core]
        out = jnp.dot(d.astype(BF16), pw_ref[k], preferred_element_type=F32) * ps_ref[:, cols]
        a_ref[:, cols] = out.astype(a_ref.dtype)

    zc = project(OFF_C, OFF_D)
    ch, cb, cc = zc[:, :D_GROUP], zc[:, D_GROUP:2 * D_GROUP], zc[:, 2 * D_GROUP:]
    c_ref[...] = (cb * _conv3_ext(cc * ch, cw_ref))[core].astype(c_ref.dtype)

    hz_ref[...] = _conv3_ext(project(OFF_D, D_IN), hw_ref)[core]


def _inmix(x2d, g, w, pool_w, pool_scale, sconv_w, hy_short_w, layer, seq):
    T = x2d.shape[0]
    tm = 512
    per = tm // HALO
    last = T // HALO - 1
    full = lambda a: pl.BlockSpec(a.shape, lambda i: (0, 0))
    row_tile = lambda width: pl.BlockSpec((tm, width), lambda i: (i, 0))
    return pl.pallas_call(
        functools.partial(_inmix_body, seq=seq),
        grid=(T // tm,),
        in_specs=[
            row_tile(D_MODEL),
            pl.BlockSpec((HALO, D_MODEL), lambda i: (jnp.maximum(i * per - 1, 0), 0)),
            pl.BlockSpec((HALO, D_MODEL), lambda i: (jnp.minimum((i + 1) * per, last), 0)),
            full(g),
            pl.BlockSpec((None, D_MODEL, D_IN), lambda i: (layer, 0, 0), pipeline_mode=pl.Buffered(1)),
            pl.BlockSpec((None, len(POOL_WINDOWS), POOL_CH, POOL_CH), lambda i: (layer, 0, 0, 0)),
            full(pool_scale), full(sconv_w), full(hy_short_w),
        ],
        out_specs=[row_tile(D_QKV), row_tile(D_GROUP), row_tile(D_GROUP), row_tile(3 * D_GROUP)],
        out_shape=[jax.ShapeDtypeStruct((T, D_QKV), F32), jax.ShapeDtypeStruct((T, D_GROUP), BF16),
                   jax.ShapeDtypeStruct((T, D_GROUP), BF16), jax.ShapeDtypeStruct((T, 3 * D_GROUP), F32)],
        compiler_params=_params(("parallel",)),
        name="inproj_local_mixers",
    )(x2d, x2d, x2d, g, w, pool_w, pool_scale, sconv_w, hy_short_w)


def _ffn_body(x_ref, g_ref, wg_ref, wu_ref, wd_ref, gf_ref, o_ref, h_ref, *, final_norm):
    j = pl.program_id(1)

    @pl.when(j == 0)
    def _():
        x = x_ref[...]
        h_ref[...] = _rms(x, g_ref[...]).astype(BF16)
        o_ref[...] = x

    h = h_ref[...]
    gate = jnp.dot(h, wg_ref[...], preferred_element_type=F32)
    up = jnp.dot(h, wu_ref[...], preferred_element_type=F32)
    act = (gate * jax.nn.sigmoid(gate) * up).astype(BF16)
    o_ref[...] += jnp.dot(act, wd_ref[...].astype(BF16), preferred_element_type=F32)

    if final_norm:
        @pl.when(j == pl.num_programs(1) - 1)
        def _():
            o_ref[...] = _rms(o_ref[...], gf_ref[...])


def _ffn(x2d, g, w_gate_up, w_down, gf, layer, final_norm):
    T = x2d.shape[0]
    tm, tf = 1024, 512
    nf = D_FF // tf
    return pl.pallas_call(
        functools.partial(_ffn_body, final_norm=final_norm),
        grid=(T // tm, nf),
        in_specs=[
            pl.BlockSpec((tm, D_MODEL), lambda i, j: (i, 0)),
            pl.BlockSpec((1, D_MODEL), lambda i, j: (0, 0)),
            pl.BlockSpec((None, D_MODEL, tf), lambda i, j: (layer, 0, j)),
            pl.BlockSpec((None, D_MODEL, tf), lambda i, j: (layer, 0, j + nf)),
            pl.BlockSpec((None, tf, D_MODEL), lambda i, j: (layer, j, 0)),
            pl.BlockSpec((1, D_MODEL), lambda i, j: (0, 0)),
        ],
        out_specs=pl.BlockSpec((tm, D_MODEL), lambda i, j: (i, 0)),
        out_shape=jax.ShapeDtypeStruct((T, D_MODEL), F32),
        scratch_shapes=[pltpu.VMEM((tm, D_MODEL), BF16)],
        compiler_params=_params(("parallel", "arbitrary"), FFN_VMEM_LIMIT),
        name="ffn",
    )(x2d, g, w_gate_up, w_gate_up, w_down, gf)


def _rope_tables(seq):
    inv_freq = ROPE_THETA ** (-np.arange(0, ROPE_DIM, 2, dtype=np.float64) / ROPE_DIM)
    ang = np.arange(seq, dtype=np.float64)[:, None] * inv_freq[None, :]
    half = ROPE_DIM // 2
    cos = np.ones((seq, HEAD_DIM))
    s_lo = np.zeros((seq, HEAD_DIM))
    s_hi = np.zeros((seq, HEAD_DIM))
    cos[:, :half] = np.cos(ang)
    cos[:, half:ROPE_DIM] = np.cos(ang)
    s_lo[:, half:ROPE_DIM] = np.sin(ang)
    s_hi[:, :half] = -np.sin(ang)
    rep = LANES // HEAD_DIM
    return tuple(jnp.asarray(np.tile(a, (1, rep)), F32) for a in (cos, s_lo, s_hi))


LOG2E = math.log2(math.e)
ATTN_QBLOCKS = 4


def _rope(x, cos, s_lo, s_hi):
    half = ROPE_DIM // 2
    return x * cos + pltpu.roll(x, half, axis=1) * s_lo + pltpu.roll(x, LANES - half, axis=1) * s_hi


def _attn_tile(q_ref, kp_ref, kc_ref, kn_ref, vp_ref, vc_ref, vn_ref, cq_ref, cp_ref, cn_ref, sink_ref, n, nstep):
    two = 2 * BLOCK
    blk_rows = lambda i: slice(i * BLOCK, (i + 1) * BLOCK)
    tq = tuple(r[...] for r in cq_ref)

    q = q_ref[...]
    qscale = HEAD_DIM ** -0.5 * LOG2E
    qg = [(_rope(q[:, g * LANES:(g + 1) * LANES], *tq) * qscale).astype(BF16) for g in range(D_GROUP // LANES)]

    lane = lax.broadcasted_iota(jnp.int32, (BLOCK, LANES), 1)
    first = lane < HEAD_DIM
    half_ones = jnp.where(first, 1.0, 0.0)
    ones_bd = jnp.concatenate([half_ones, 1.0 - half_ones], axis=0)

    def per_head(x):
        xr = pltpu.roll(x, HEAD_DIM, axis=1)
        return [jnp.concatenate([jnp.where(first, x, 0.0), jnp.where(first, 0.0, xr)], axis=0),
                jnp.concatenate([jnp.where(first, xr, 0.0), jnp.where(first, 0.0, x)], axis=0)]

    kc = _rope(kc_ref[...], *tq)
    vc = vc_ref[...]
    kblocks = ([_rope(kp_ref[...], *(r[...] for r in cp_ref))] + [kc[blk_rows(i)] for i in range(ATTN_QBLOCKS)]
               + [_rope(kn_ref[...], *(r[...] for r in cn_ref))])
    vblocks = [vp_ref[...]] + [vc[blk_rows(i)] for i in range(ATTN_QBLOCKS)] + [vn_ref[...]]
    kcat = [[a.astype(BF16) for a in per_head(k)] for k in kblocks]
    vcat = [[jnp.concatenate([a, ones_bd], axis=1).astype(BF16) for a in per_head(v)] for v in vblocks]

    r = lax.broadcasted_iota(jnp.int32, (two, two), 0) % BLOCK
    c = lax.broadcasted_iota(jnp.int32, (two, two), 1)
    left = c < BLOCK
    c = c % BLOCK
    below, above = c >= r, c <= r
    first2 = lax.broadcasted_iota(jnp.int32, (two, LANES), 1) < HEAD_DIM

    tiles = []
    for i in range(ATTN_QBLOCKS):
        lo_mask = (below & (n > 0)) if i == 0 else below
        hi_mask = (above & (n < nstep - 1)) if i == ATTN_QBLOCKS - 1 else above
        masks = [lo_mask, None, hi_mask]
        outs = []
        for kv in range(N_KV_HEADS):
            qs = jnp.concatenate([qg[2 * kv][blk_rows(i)], qg[2 * kv + 1][blk_rows(i)]], axis=0)
            s = []
            for j in range(3):
                sb = lax.dot_general(qs, kcat[i + j][kv], (((1,), (1,)), ((), ())), preferred_element_type=F32)
                if masks[j] is not None:
                    sb = jnp.where(masks[j], sb, -jnp.inf)
                s.append(sb)
            sink = sink_ref[kv]
            top = jnp.maximum(jnp.maximum(s[0], s[1]), s[2])
            m = jnp.where(left, jnp.max(top[:, :BLOCK], axis=-1, keepdims=True),
                          jnp.max(top[:, BLOCK:], axis=-1, keepdims=True))
            m = jnp.maximum(m, sink)
            acc = sum(jnp.dot(jnp.exp2(s[j] - m).astype(BF16), vcat[i + j][kv], preferred_element_type=F32)
                      for j in range(3))
            m_out = jnp.where(first2, m[:, :LANES], m[:, LANES:])
            sink_out = jnp.where(first2, sink[:, :LANES], sink[:, LANES:])
            o = acc[:, :LANES] / (acc[:, LANES:] + jnp.exp2(sink_out - m_out))
            outs += [o[:BLOCK], o[BLOCK:]]
        tiles.append(jnp.concatenate(outs, axis=1).astype(BF16))
    return jnp.concatenate(tiles, axis=0)


def _attn_outproj_body(x_ref, a_ref, c_ref, d_ref, w_ref, *rest, nstep):
    attn_refs, o_ref = rest[:-1], rest[-1]
    acc = x_ref[...]
    for k, m_ref in ((0, a_ref), (2, c_ref), (3, d_ref)):
        acc = acc + jnp.dot(m_ref[...], w_ref[k * D_GROUP:(k + 1) * D_GROUP, :], preferred_element_type=F32)
    b = _attn_tile(*attn_refs, pl.program_id(0) % nstep, nstep)
    o_ref[...] = acc + jnp.dot(b, w_ref[D_GROUP:2 * D_GROUP, :], preferred_element_type=F32)


def _attn_outproj(x2d, qkv, sink, out_a, out_c, out_d, w, layer, seq):
    T = x2d.shape[0]
    rows = ATTN_QBLOCKS * BLOCK
    nblk = seq // BLOCK
    nstep = seq // rows
    last_blk = T // BLOCK - 1
    tabs = _rope_tables(seq)
    sink_tab = jnp.repeat(jnp.repeat((sink.astype(F32) * LOG2E).reshape(N_KV_HEADS, 2, 2), BLOCK, axis=1),
                          BLOCK, axis=2)
    kcol, vcol = (OFF_K - OFF_Q) // LANES, (OFF_V - OFF_Q) // LANES

    prev = lambda i: jnp.maximum(i * ATTN_QBLOCKS - 1, 0)
    nxt = lambda i: jnp.minimum((i + 1) * ATTN_QBLOCKS, last_blk)

    def own(width, col=0):
        return pl.BlockSpec((rows, width), lambda i: (i, col))

    def edge(col, sel):
        return pl.BlockSpec((BLOCK, LANES), lambda i: (sel(i), col))

    def tab_edge(sel):
        return [pl.BlockSpec((BLOCK, LANES), lambda i: (sel(i) % nblk, 0))] * 3

    return pl.pallas_call(
        functools.partial(_attn_outproj_body, nstep=nstep),
        grid=(T // rows,),
        in_specs=[
            own(D_MODEL), own(D_GROUP), own(D_GROUP), own(D_GROUP),
            pl.BlockSpec((None, D_MODEL, D_MODEL), lambda i: (layer, 0, 0), pipeline_mode=pl.Buffered(1)),
            own(D_GROUP),
            edge(kcol, prev), own(LANES, kcol), edge(kcol, nxt),
            edge(vcol, prev), own(LANES, vcol), edge(vcol, nxt),
            [pl.BlockSpec((rows, LANES), lambda i: (i % nstep, 0))] * 3, tab_edge(prev), tab_edge(nxt),
            pl.BlockSpec((N_KV_HEADS, 2 * BLOCK, 2 * BLOCK), lambda i: (0, 0, 0)),
        ],
        out_specs=own(D_MODEL),
        out_shape=jax.ShapeDtypeStruct((T, D_MODEL), F32),
        compiler_params=_params(("parallel",)),
        name="attention_outproj",
    )(x2d, out_a, out_c, out_d, w, qkv, qkv, qkv, qkv, qkv, qkv, qkv, list(tabs), list(tabs), list(tabs), sink_tab)


def _is(x, v):
    return abs(x - v) < 1e-12


def _scale(a, s):
    if a is None or _is(s, 0.0):
        return None
    if _is(s, 1.0):
        return a
    if _is(s, -1.0):
        return -a
    return a * s


def _cmul_const(a, w):
    ar, ai = a
    return (_psub(_scale(ar, w.real), _scale(ai, w.imag)), _padd(_scale(ar, w.imag), _scale(ai, w.real)))


def _padd(x, y):
    if x is None:
        return y
    if y is None:
        return x
    return x + y


def _psub(x, y):
    if y is None:
        return x
    if x is None:
        return -y
    return x - y


def _fft_blocks(xs, sign, leaves_done=False):
    n = len(xs)
    if n == 1 or (n == 2 and leaves_done):
        return list(xs)
    ev = _fft_blocks(xs[0::2], sign, leaves_done)
    od = _fft_blocks(xs[1::2], sign, leaves_done)
    out = [None] * n
    for k in range(n // 2):
        t = _cmul_const(od[k], np.exp(sign * 2j * math.pi * k / n))
        out[k] = (_padd(ev[k][0], t[0]), _padd(ev[k][1], t[1]))
        out[k + n // 2] = (_psub(ev[k][0], t[0]), _psub(ev[k][1], t[1]))
    return out


def _dense(a, like):
    return jnp.zeros_like(like) if a is None else a


def _dft_stack():
    k = np.arange(DFT_N, dtype=np.float64)
    ang = 2.0 * math.pi * np.outer(k, k) / DFT_N
    c, s = np.cos(ang), np.sin(ang)
    return jnp.asarray(np.stack([np.concatenate([c, -s], axis=1), np.concatenate([c, s], axis=1)]), F32)


def _twiddle_table(n1f):
    f1 = np.arange(n1f, dtype=np.float64)[:, None]
    t2 = np.arange(DFT_N, dtype=np.float64)[None, :]
    ang = (-2.0 * math.pi / (n1f * DFT_N)) * f1 * t2
    rep = lambda a: jnp.asarray(np.repeat(a.reshape(-1, 1), LANES, axis=1), F32)
    return rep(np.cos(ang)), rep(np.sin(ang))


def _block_dft(f, qr, qi):
    qr, qi = qr.astype(BF16), qi.astype(BF16)
    rhs = jnp.concatenate([jnp.concatenate([qr, qi], axis=1), jnp.concatenate([-qi, qr], axis=1)], axis=0)
    d = jnp.dot(f, rhs, preferred_element_type=F32)
    return d[:, 0:LANES], d[:, LANES:]


def _forward_rows(xs, n1f):
    half = n1f // 2
    out = [None] * n1f
    if len(xs) == n1f:
        out[:half] = _fft_blocks(xs[0::2], -1.0)
        out[half:] = _fft_blocks(xs[1::2], -1.0)
        return out
    branches = (xs, [_cmul_const(x, np.exp(-2j * math.pi * t / n1f)) for t, x in enumerate(xs)])
    for b, xb in enumerate(branches):
        out[b:half:2] = _fft_blocks(xb[0::2], -1.0)
        out[half + b::2] = _fft_blocks(xb[1::2], -1.0)
    return out


def _pair_factors(n1f, pruned):
    f = np.arange(n1f // 2, dtype=np.float64)
    ang = -2.0 * math.pi * ((f // 2) / (n1f // 2) if pruned else f / n1f)
    return jnp.asarray(np.cos(ang), F32), jnp.asarray(np.sin(ang), F32)


MID_BLOCKS = 8
FILTER_MID_BLOCKS = 8


def _block_rows(f):
    start = f * DFT_N
    return pl.ds(start if isinstance(start, int) else pl.multiple_of(start, DFT_N), DFT_N)


def _store_rows(q_scr, blocks, r, like):
    for f, (pr, pi) in enumerate(blocks):
        rows = pl.ds(f * DFT_N + r, SUBLANES)
        q_scr[rows, 0:LANES] = _dense(pr, like)
        q_scr[rows, LANES:] = _dense(pi, like)


def _pair_spectra(fst, q_scr, twr_ref, twi_ref, cr_ref, ci_ref, f, half):
    lo, hi = _block_rows(f), _block_rows(f + half)
    er, ei = q_scr[lo, 0:LANES], q_scr[lo, LANES:]
    orr, oi = q_scr[hi, 0:LANES], q_scr[hi, LANES:]
    cr, ci = cr_ref[f], ci_ref[f]
    wr, wi = orr * cr - oi * ci, orr * ci + oi * cr
    out = []
    for rows, pr, pi in ((lo, er + wr, ei + wi), (hi, er - wr, ei - wi)):
        tr, ti = twr_ref[rows, :], twi_ref[rows, :]
        out.append((rows, _block_dft(fst[0], pr * tr - pi * ti, pr * ti + pi * tr)))
    return out


def _hconv_body(u_ref, g_ref, kr_ref, ki_ref, twr_ref, twi_ref, cr_ref, ci_ref, fst_ref, b_ref, o_ref, q_scr,
                *, seq):
    n1 = seq // DFT_N
    n1f = 2 * n1
    fst = fst_ref[...]

    def fwd(i, carry):
        r = pl.multiple_of(i * SUBLANES, SUBLANES)
        xs = [(u_ref[pl.ds(t * DFT_N + r, SUBLANES), :], u_ref[pl.ds(seq + t * DFT_N + r, SUBLANES), :])
              for t in range(n1)]
        _store_rows(q_scr, _forward_rows(xs, n1f), r, xs[0][0])
        return carry

    lax.fori_loop(0, DFT_N // SUBLANES, fwd, 0)

    def mid(i, carry):
        pairs = [_pair_spectra(fst, q_scr, twr_ref, twi_ref, cr_ref, ci_ref, i * (MID_BLOCKS // 2) + k, n1)
                 for k in range(MID_BLOCKS // 2)]
        s = []
        for pair in pairs:
            sp = []
            for r, (xr, xi) in pair:
                kr, ki = kr_ref[r, :].astype(F32), ki_ref[r, :].astype(F32)
                rr, ri = _block_dft(fst[1], xr * kr - xi * ki, xr * ki + xi * kr)
                tr, ti = twr_ref[r, :], twi_ref[r, :]
                sp.append((rr * tr + ri * ti, ri * tr - rr * ti))
            (lo, _), (hi, _) = pair
            s.append((lo, sp[0][0] + sp[1][0], sp[0][1] + sp[1][1]))
            s.append((hi, sp[0][0] - sp[1][0], sp[0][1] - sp[1][1]))
        for r, sr_, si_ in s:
            q_scr[r, 0:LANES] = sr_
            q_scr[r, LANES:] = si_
        return carry

    lax.fori_loop(0, n1f // MID_BLOCKS, mid, 0)

    bias = b_ref[...]

    def inv(i, carry):
        r = pl.multiple_of(i * SUBLANES, SUBLANES)
        ss = [(q_scr[pl.ds(f * DFT_N + r, SUBLANES), 0:LANES], q_scr[pl.ds(f * DFT_N + r, SUBLANES), LANES:])
              for f in range(n1f)]
        ev = _fft_blocks(ss[0::2], 1.0, leaves_done=True)
        od = _fft_blocks(ss[1::2], 1.0, leaves_done=True)
        for t in range(n1):
            o = _cmul_const(od[t], np.exp(2j * math.pi * t / n1f))
            for b, y in enumerate((ev[t][0] + o[0], ev[t][1] + o[1])):
                rows = pl.ds(b * seq + t * DFT_N + r, SUBLANES)
                o_ref[rows, :] = (g_ref[rows, :] * (y + bias * u_ref[rows, :])).astype(o_ref.dtype)
        return carry

    lax.fori_loop(0, DFT_N // SUBLANES, inv, 0)


def _hconv(u, ucol, g, gcol, spec_r, spec_i, order, tw, fst, bias, nb, seq, out_dtype):
    n1f = 2 * seq // DFT_N
    nc = D_GROUP // LANES
    once = pl.Buffered(1)
    return pl.pallas_call(
        functools.partial(_hconv_body, seq=seq),
        grid=(nc, nb // 2),
        in_specs=[
            pl.BlockSpec((2 * seq, LANES), lambda c, p: (p, ucol + c)),
            pl.BlockSpec((2 * seq, LANES), lambda c, p: (p, gcol + c)),
            pl.BlockSpec((None, n1f * DFT_N, LANES), lambda c, p: (order, 0, c)),
            pl.BlockSpec((None, n1f * DFT_N, LANES), lambda c, p: (order, 0, c)),
            pl.BlockSpec((n1f * DFT_N, LANES), lambda c, p: (0, 0), pipeline_mode=once),
            pl.BlockSpec((n1f * DFT_N, LANES), lambda c, p: (0, 0), pipeline_mode=once),
            pl.BlockSpec(memory_space=pltpu.SMEM), pl.BlockSpec(memory_space=pltpu.SMEM),
            pl.BlockSpec((2, DFT_N, 2 * DFT_N), lambda c, p: (0, 0, 0)),
            pl.BlockSpec((1, LANES), lambda c, p: (0, c)),
        ],
        out_specs=pl.BlockSpec((2 * seq, LANES), lambda c, p: (p, c)),
        out_shape=jax.ShapeDtypeStruct((nb * seq, D_GROUP), out_dtype),
        scratch_shapes=[pltpu.VMEM((n1f * DFT_N, 2 * LANES), F32)],
        compiler_params=_params(("arbitrary", "arbitrary")),
        name="hyena_long_conv",
    )(u, g, spec_r, spec_i, tw[0], tw[1], *_pair_factors(n1f, True), fst, bias[order][None, :])


def _hidden_body(f_ref, w1_ref, b1_ref, w2_ref, b2_ref, fr_ref, o_ref):
    hi = lax.Precision.HIGHEST
    fr = fr_ref[...]
    h = jnp.sin(fr * (jnp.dot(f_ref[...], w1_ref[...], precision=hi, preferred_element_type=F32) + b1_ref[...]))
    o_ref[...] = jnp.sin(fr * (jnp.dot(h, w2_ref[...], precision=hi, preferred_element_type=F32) + b2_ref[...]))


def _filter_features(seq):
    bands = np.linspace(1e-4, HY_BANDS - 1, HY_BANDS)[None, :]

    def feats(j):
        wpos = (2.0 * math.pi / seq) * j
        return np.concatenate([j / (seq - 1.0), np.cos(bands * wpos), -np.sin(bands * wpos)], axis=-1)

    j = np.arange(seq, dtype=np.float64)[:, None]
    return jnp.asarray(np.concatenate([feats(j), feats(seq - j)], axis=1), F32)


def _blockdiag2(w):
    z = jnp.zeros_like(w)
    return jnp.concatenate([jnp.concatenate([w, z], axis=1), jnp.concatenate([z, w], axis=1)], axis=0)


def _filter_hidden(feats, w1, b1, w2, b2, freq):
    seq = feats.shape[0]
    two = lambda v: jnp.concatenate([v, v])[None, :]
    tr = 512
    full = lambda a: pl.BlockSpec(a.shape, lambda i: (0, 0))
    args = (_blockdiag2(w1), two(b1), _blockdiag2(w2), two(b2), two(freq))
    return pl.pallas_call(
        _hidden_body,
        grid=(seq // tr,),
        in_specs=[pl.BlockSpec((tr, 2 * HY_EMB), lambda i: (i, 0))] + [full(a) for a in args],
        out_specs=pl.BlockSpec((tr, 2 * HY_HIDDEN), lambda i: (i, 0)),
        out_shape=jax.ShapeDtypeStruct((seq, 2 * HY_HIDDEN), F32),
        compiler_params=_params(("parallel",)),
        name="hyena_filter_hidden",
    )(feats, *args)


FILTER_ROWS = 512


def _split_bf16(a):
    hi = a.astype(BF16)
    return hi, (a - hi.astype(F32)).astype(BF16)


def _hfilter_body(h_ref, w3_ref, d_ref, twr_ref, twi_ref, cr_ref, ci_ref, fst_ref, or_ref, oi_ref, k_scr, q_scr,
                  *, seq):
    n1f = 2 * seq // DFT_N
    dec = jnp.abs(d_ref[...])
    w_hi, w_lo = _split_bf16(w3_ref[...])
    norm = jnp.zeros((1, LANES), F32)
    for r0 in range(0, seq, FILTER_ROWS):
        j = (lax.broadcasted_iota(jnp.int32, (FILTER_ROWS, 1), 0) + r0).astype(F32)
        h_hi, h_lo = _split_bf16(h_ref[r0:r0 + FILTER_ROWS, :])
        k2 = (jnp.dot(h_hi, w_hi, preferred_element_type=F32) + jnp.dot(h_hi, w_lo, preferred_element_type=F32)
              + jnp.dot(h_lo, w_hi, preferred_element_type=F32))
        kf = k2[:, :LANES] * jnp.exp(-(j / (seq - 1.0)) * dec[:, :LANES])
        kb = k2[:, LANES:] * jnp.exp(-((seq - j) / (seq - 1.0)) * dec[:, LANES:])
        kb = jnp.where(j > 0.0, kb, 0.0)
        norm = norm + jnp.sum(jnp.abs(kf), axis=0, keepdims=True) + jnp.sum(jnp.abs(kb), axis=0, keepdims=True)
        k_scr[r0:r0 + FILTER_ROWS, :] = kf
        k_scr[seq + r0:seq + r0 + FILTER_ROWS, :] = kb
    scale = (1.0 / (2 * seq)) / norm
    for r0 in range(0, 2 * seq, FILTER_ROWS):
        k_scr[r0:r0 + FILTER_ROWS, :] = k_scr[r0:r0 + FILTER_ROWS, :] * scale
    fst = fst_ref[...]

    def fwd(i, carry):
        r = pl.multiple_of(i * SUBLANES, SUBLANES)
        xs = [(k_scr[pl.ds(t1 * DFT_N + r, SUBLANES), :], None) for t1 in range(n1f)]
        _store_rows(q_scr, _forward_rows(xs, n1f), r, xs[0][0])
        return carry

    lax.fori_loop(0, DFT_N // SUBLANES, fwd, 0)

    def mid(i, carry):
        pairs = [_pair_spectra(fst, q_scr, twr_ref, twi_ref, cr_ref, ci_ref, i * (FILTER_MID_BLOCKS // 2) + k,
                               n1f // 2) for k in range(FILTER_MID_BLOCKS // 2)]
        for pair in pairs:
            for r, (xr, xi) in pair:
                or_ref[r, :] = xr.astype(or_ref.dtype)
                oi_ref[r, :] = xi.astype(oi_ref.dtype)
        return carry

    lax.fori_loop(0, n1f // FILTER_MID_BLOCKS, mid, 0)


def _by_block(a):
    nc = D_GROUP // LANES
    lead = a.shape[:-1]
    a = a.reshape(lead + (HY_ORDER, 2, nc, LANES))
    return jnp.swapaxes(a, -3, -2).reshape(lead + (HY_ORDER * nc * 2 * LANES,))


def _hfilter(hid, w3, decay, tw, fst, seq):
    n1f = 2 * seq // DFT_N
    nc = D_GROUP // LANES
    direction = jnp.arange(w3.shape[1]) // D_GROUP % 2
    w3 = jnp.concatenate([jnp.where(direction == 0, w3, 0.0), jnp.where(direction == 1, w3, 0.0)], axis=0)
    w3 = _by_block(w3)
    dec = _by_block(decay)[None, :]
    once = pl.Buffered(1)
    pair = lambda o, c: (0, nc * o + c)
    out_spec = pl.BlockSpec((None, n1f * DFT_N, LANES), lambda o, c: (o, 0, c))
    out_shape = jax.ShapeDtypeStruct((HY_ORDER, n1f * DFT_N, D_GROUP), BF16)
    return pl.pallas_call(
        functools.partial(_hfilter_body, seq=seq),
        grid=(HY_ORDER, nc),
        in_specs=[
            pl.BlockSpec((seq, 2 * HY_HIDDEN), lambda o, c: (0, 0)),
            pl.BlockSpec((2 * HY_HIDDEN, 2 * LANES), pair),
            pl.BlockSpec((1, 2 * LANES), pair),
            pl.BlockSpec((n1f * DFT_N, LANES), lambda o, c: (0, 0), pipeline_mode=once),
            pl.BlockSpec((n1f * DFT_N, LANES), lambda o, c: (0, 0), pipeline_mode=once),
            pl.BlockSpec(memory_space=pltpu.SMEM), pl.BlockSpec(memory_space=pltpu.SMEM),
            pl.BlockSpec((2, DFT_N, 2 * DFT_N), lambda o, c: (0, 0, 0)),
        ],
        out_specs=[out_spec, out_spec],
        out_shape=[out_shape, out_shape],
        scratch_shapes=[pltpu.VMEM((2 * seq, LANES), F32), pltpu.VMEM((n1f * DFT_N, 2 * LANES), F32)],
        compiler_params=_params(("arbitrary", "arbitrary")),
        name="hyena_filter_spectrum",
    )(hid, w3, dec, tw[0], tw[1], *_pair_factors(n1f, False), fst)


def _trunk(x, p, fst):
    nb, seq, _ = x.shape
    x2d = x.reshape(nb * seq, D_MODEL)
    tw = _twiddle_table(2 * seq // DFT_N)
    feats = _filter_features(seq)
    vcol = 0
    g1col = D_GROUP // LANES
    g2col = 2 * D_GROUP // LANES
    for l in range(DEPTH):
        qkv, out_a, out_c, hz = _inmix(x2d, p["mix_norm_g"][l][None, :], p["w_in"], p["pool_w"],
                                       p["pool_scale"][l][None, :], p["sconv_w"][l], p["hy_short_w"][l], l, seq)
        hid = _filter_hidden(feats, p["hy_w1"][l], p["hy_b1"][l], p["hy_w2"][l], p["hy_b2"][l], p["hy_freq"][l])
        spec_r, spec_i = _hfilter(hid, p["hy_w3"][l], p["hy_decay"][l], tw, fst, seq)
        bias = p["hy_bias"][l]
        u1 = _hconv(hz, vcol, hz, g1col, spec_r, spec_i, 0, tw, fst, bias, nb, seq, F32)
        out_d = _hconv(u1, 0, hz, g2col, spec_r, spec_i, 1, tw, fst, bias, nb, seq, BF16)
        x2d = _attn_outproj(x2d, qkv, p["attn_sink"][l], out_a, out_c, out_d, p["w_out"], l, seq)
        x2d = _ffn(x2d, p["ffn_norm_g"][l][None, :], p["w_gate_up"], p["w_down"],
                   p["final_norm_g"][None, :], l, final_norm=(l == DEPTH - 1))
    return x2d.reshape(nb, seq, D_MODEL)


def kernel(x_prompt, x_sample, mix_norm_g, w_in, pool_w, pool_scale, attn_sink, sconv_w, hy_short_w, hy_w1, hy_b1, hy_w2, hy_b2, hy_w3, hy_freq, hy_decay, hy_bias, w_out, ffn_norm_g, w_gate_up, w_down, final_norm_g):
    p = dict(
        mix_norm_g=mix_norm_g, w_in=w_in.astype(BF16), pool_w=pool_w.astype(BF16), pool_scale=pool_scale,
        attn_sink=attn_sink, sconv_w=sconv_w, hy_short_w=hy_short_w, hy_w1=hy_w1, hy_b1=hy_b1, hy_w2=hy_w2,
        hy_b2=hy_b2, hy_w3=hy_w3, hy_freq=hy_freq, hy_decay=hy_decay, hy_bias=hy_bias,
        w_out=w_out.astype(BF16), ffn_norm_g=ffn_norm_g, w_gate_up=w_gate_up.astype(BF16),
        w_down=w_down, final_norm_g=final_norm_g,
    )
    fst = _dft_stack().astype(BF16)
    return (_trunk(x_prompt, p, fst), _trunk(x_sample, p, fst))
```

```python
import functools
import math

import numpy as np
import jax
import jax.numpy as jnp
from jax import lax
from jax.experimental import pallas as pl
from jax.experimental.pallas import tpu as pltpu

F32 = jnp.float32
BF16 = jnp.bfloat16

D_MODEL = 2048
DEPTH = 2
D_GROUP = 512
POOL_WINDOWS = (2, 4, 8, 16)
POOL_CH = 128
HEAD_DIM = 64
N_Q_HEADS = 8
N_KV_HEADS = 2
GQA_GROUP = 4
BLOCK = 128
ROPE_DIM = 16
ROPE_THETA = 500000.0
HY_ORDER = 2
HY_EMB = 33
HY_BANDS = 16
HY_HIDDEN = 64
D_FF = 5632
NORM_EPS = 1e-6
OFF_Q = 512
OFF_K = 1024
OFF_V = 1152
OFF_C = 1280
OFF_D = 2816
D_IN = 4352

LANES = 128
SUBLANES = 8
DFT_N = 256
VMEM_LIMIT = 56 * 2 ** 20
FFN_VMEM_LIMIT = 60 * 2 ** 20


def _params(sem, vmem=VMEM_LIMIT):
    return pltpu.CompilerParams(dimension_semantics=sem, vmem_limit_bytes=vmem)


def _rms(x, g):
    ms = jnp.mean(x * x, axis=-1, keepdims=True)
    return x * lax.rsqrt(ms + NORM_EPS) * g


HALO = SUBLANES
D_QKV = OFF_C - OFF_Q


def _conv3_ext(z, w_ref):
    n = z.shape[0]
    return (pltpu.roll(z, 1, axis=0) * w_ref[0:1, :] + z * w_ref[1:2, :]
            + pltpu.roll(z, n - 1, axis=0) * w_ref[2:3, :])


def _inmix_body(x_ref, xp_ref, xn_ref, g_ref, w_ref, pw_ref, ps_ref, cw_ref, hw_ref,
                qkv_ref, a_ref, c_ref, hz_ref, *, seq):
    tm = x_ref.shape[0]
    ext = tm + 2 * HALO
    core = slice(HALO, HALO + tm)
    g = g_ref[...]
    pos0 = (pl.program_id(0) * tm) % seq
    before = jnp.where(pos0 > 0, _rms(xp_ref[...], g), 0.0)
    after = jnp.where(pos0 + tm < seq, _rms(xn_ref[...], g), 0.0)
    hn = jnp.concatenate([before, _rms(x_ref[...], g), after], axis=0).astype(BF16)
    row = lax.broadcasted_iota(jnp.int32, (ext, 1), 0)

    def project(lo, hi):
        return jnp.dot(hn, w_ref[:, lo:hi], preferred_element_type=F32)

    qkv_ref[...] = project(OFF_Q, OFF_C)[core]

    za = project(0, OFF_Q)
    pos = pos0 + row - HALO
    for k, win in enumerate(POOL_WINDOWS):
        cols = slice(k * POOL_CH, (k + 1) * POOL_CH)
        p = za[:, cols]
        s = p + pltpu.roll(p, 1, axis=0)
        step = 1
        while 2 * step < win:
            s = pltpu.roll(s, step, axis=0) + pltpu.roll(s, ext - step, axis=0)
            step *= 2
        half = win // 2
        cnt = jnp.maximum(jnp.minimum(pos + half, seq) - jnp.maximum(pos - half, 0), 1).astype(F32)
        d = (s / cnt - p)[core]
        out = jnp.dot(d.astype(BF16), pw_ref[k], preferred_element_type=F32) * ps_ref[:, cols]
        a_ref[:, cols] = out.astype(a_ref.dtype)

    zc = project(OFF_C, OFF_D)
    ch, cb, cc = zc[:, :D_GROUP], zc[:, D_GROUP:2 * D_GROUP], zc[:, 2 * D_GROUP:]
    c_ref[...] = (cb * _conv3_ext(cc * ch, cw_ref))[core].astype(c_ref.dtype)

    hz_ref[...] = _conv3_ext(project(OFF_D, D_IN), hw_ref)[core]


def _inmix(x2d, g, w, pool_w, pool_scale, sconv_w, hy_short_w, layer, seq):
    T = x2d.shape[0]
    tm = 512
    per = tm // HALO
    last = T // HALO - 1
    full = lambda a: pl.BlockSpec(a.shape, lambda i: (0, 0))
    row_tile = lambda width: pl.BlockSpec((tm, width), lambda i: (i, 0))
    return pl.pallas_call(
        functools.partial(_inmix_body, seq=seq),
        grid=(T // tm,),
        in_specs=[
            row_tile(D_MODEL),
            pl.BlockSpec((HALO, D_MODEL), lambda i: (jnp.maximum(i * per - 1, 0), 0)),
            pl.BlockSpec((HALO, D_MODEL), lambda i: (jnp.minimum((i + 1) * per, last), 0)),
            full(g),
            pl.BlockSpec((None, D_MODEL, D_IN), lambda i: (layer, 0, 0), pipeline_mode=pl.Buffered(1)),
            pl.BlockSpec((None, len(POOL_WINDOWS), POOL_CH, POOL_CH), lambda i: (layer, 0, 0, 0)),
            full(pool_scale), full(sconv_w), full(hy_short_w),
        ],
        out_specs=[row_tile(D_QKV), row_tile(D_GROUP), row_tile(D_GROUP), row_tile(3 * D_GROUP)],
        out_shape=[jax.ShapeDtypeStruct((T, D_QKV), F32), jax.ShapeDtypeStruct((T, D_GROUP), BF16),
                   jax.ShapeDtypeStruct((T, D_GROUP), BF16), jax.ShapeDtypeStruct((T, 3 * D_GROUP), F32)],
        compiler_params=_params(("parallel",)),
        name="inproj_local_mixers",
    )(x2d, x2d, x2d, g, w, pool_w, pool_scale, sconv_w, hy_short_w)


def _ffn_body(x_hbm, g_ref, wg_ref, wu_ref, wd_ref, gf_ref, o_ref, h_ref, x_scr, sem, *, final_norm):
    i, j = pl.program_id(0), pl.program_id(1)
    tm = x_scr.shape[0]

    def x_copy(tile):
        return pltpu.make_async_copy(x_hbm.at[pl.ds(pl.multiple_of(tile * tm, tm), tm), :], x_scr, sem)

    @pl.when((i == 0) & (j == 0))
    def _():
        x_copy(0).start()

    @pl.when(j == 0)
    def _():
        x_copy(i).wait()
        x = x_scr[...]
        h_ref[...] = _rms(x, g_ref[...]).astype(BF16)
        o_ref[...] = x

        @pl.when(i + 1 < pl.num_programs(0))
        def _():
            x_copy(i + 1).start()

    h = h_ref[...]
    gate = jnp.dot(h, wg_ref[...].astype(BF16), preferred_element_type=F32)
    up = jnp.dot(h, wu_ref[...].astype(BF16), preferred_element_type=F32)
    act = (gate * jax.nn.sigmoid(gate) * up).astype(BF16)
    o_ref[...] += jnp.dot(act, wd_ref[...].astype(BF16), preferred_element_type=F32)

    if final_norm:
        @pl.when(j == pl.num_programs(1) - 1)
        def _():
            o_ref[...] = _rms(o_ref[...], gf_ref[...])


def _ffn(x2d, g, w_gate_up, w_down, gf, layer, final_norm):
    T = x2d.shape[0]
    tm, tf = 1024, 512
    nf = D_FF // tf
    return pl.pallas_call(
        functools.partial(_ffn_body, final_norm=final_norm),
        grid=(T // tm, nf),
        in_specs=[
            pl.BlockSpec(memory_space=pl.ANY),
            pl.BlockSpec((1, D_MODEL), lambda i, j: (0, 0)),
            pl.BlockSpec((None, D_MODEL, tf), lambda i, j: (layer, 0, j)),
            pl.BlockSpec((None, D_MODEL, tf), lambda i, j: (layer, 0, j + nf)),
            pl.BlockSpec((None, tf, D_MODEL), lambda i, j: (layer, j, 0)),
            pl.BlockSpec((1, D_MODEL), lambda i, j: (0, 0)),
        ],
        out_specs=pl.BlockSpec((tm, D_MODEL), lambda i, j: (i, 0)),
        out_shape=jax.ShapeDtypeStruct((T, D_MODEL), F32),
        scratch_shapes=[pltpu.VMEM((tm, D_MODEL), BF16), pltpu.VMEM((tm, D_MODEL), F32),
                        pltpu.SemaphoreType.DMA(())],
        compiler_params=_params(("arbitrary", "arbitrary"), FFN_VMEM_LIMIT),
        name="ffn",
    )(x2d, g, w_gate_up, w_gate_up, w_down, gf)


def _rope_tables(seq):
    inv_freq = ROPE_THETA ** (-np.arange(0, ROPE_DIM, 2, dtype=np.float64) / ROPE_DIM)
    ang = np.arange(seq, dtype=np.float64)[:, None] * inv_freq[None, :]
    half = ROPE_DIM // 2
    cos = np.ones((seq, HEAD_DIM))
    s_lo = np.zeros((seq, HEAD_DIM))
    s_hi = np.zeros((seq, HEAD_DIM))
    cos[:, :half] = np.cos(ang)
    cos[:, half:ROPE_DIM] = np.cos(ang)
    s_lo[:, half:ROPE_DIM] = np.sin(ang)
    s_hi[:, :half] = -np.sin(ang)
    rep = LANES // HEAD_DIM
    return tuple(jnp.asarray(np.tile(a, (1, rep)), F32) for a in (cos, s_lo, s_hi))


LOG2E = math.log2(math.e)
ATTN_QBLOCKS = 4


def _rope(x, cos, s_lo, s_hi):
    half = ROPE_DIM // 2
    return x * cos + pltpu.roll(x, half, axis=1) * s_lo + pltpu.roll(x, LANES - half, axis=1) * s_hi


def _attn_tile(q_ref, kp_ref, kc_ref, kn_ref, vp_ref, vc_ref, vn_ref, cq_ref, cp_ref, cn_ref, sink_ref, n, nstep):
    two = 2 * BLOCK
    blk_rows = lambda i: slice(i * BLOCK, (i + 1) * BLOCK)
    tq = tuple(r[...] for r in cq_ref)

    q = q_ref[...]
    qscale = HEAD_DIM ** -0.5 * LOG2E
    qg = [(_rope(q[:, g * LANES:(g + 1) * LANES], *tq) * qscale).astype(BF16) for g in range(D_GROUP // LANES)]

    lane = lax.broadcasted_iota(jnp.int32, (BLOCK, LANES), 1)
    first = lane < HEAD_DIM
    half_ones = jnp.where(first, 1.0, 0.0)
    ones_bd = jnp.concatenate([half_ones, 1.0 - half_ones], axis=0)

    def per_head(x):
        xr = pltpu.roll(x, HEAD_DIM, axis=1)
        return [jnp.concatenate([jnp.where(first, x, 0.0), jnp.where(first, 0.0, xr)], axis=0),
                jnp.concatenate([jnp.where(first, xr, 0.0), jnp.where(first, 0.0, x)], axis=0)]

    kc = _rope(kc_ref[...], *tq)
    vc = vc_ref[...]
    kblocks = ([_rope(kp_ref[...], *(r[...] for r in cp_ref))] + [kc[blk_rows(i)] for i in range(ATTN_QBLOCKS)]
               + [_rope(kn_ref[...], *(r[...] for r in cn_ref))])
    vblocks = [vp_ref[...]] + [vc[blk_rows(i)] for i in range(ATTN_QBLOCKS)] + [vn_ref[...]]
    kcat = [[a.astype(BF16) for a in per_head(k)] for k in kblocks]
    vcat = [[jnp.concatenate([a, ones_bd], axis=1).astype(BF16) for a in per_head(v)] for v in vblocks]

    r = lax.broadcasted_iota(jnp.int32, (two, two), 0) % BLOCK
    c = lax.broadcasted_iota(jnp.int32, (two, two), 1)
    left = c < BLOCK
    c = c % BLOCK
    below, above = c >= r, c <= r
    first2 = lax.broadcasted_iota(jnp.int32, (two, LANES), 1) < HEAD_DIM

    tiles = []
    for i in range(ATTN_QBLOCKS):
        lo_mask = (below & (n > 0)) if i == 0 else below
        hi_mask = (above & (n < nstep - 1)) if i == ATTN_QBLOCKS - 1 else above
        masks = [lo_mask, None, hi_mask]
        outs = []
        for kv in range(N_KV_HEADS):
            qs = jnp.concatenate([qg[2 * kv][blk_rows(i)], qg[2 * kv + 1][blk_rows(i)]], axis=0)
            s = []
            for j in range(3):
                sb = lax.dot_general(qs, kcat[i + j][kv], (((1,), (1,)), ((), ())), preferred_element_type=F32)
                if masks[j] is not None:
                    sb = jnp.where(masks[j], sb, -jnp.inf)
                s.append(sb)
            sink = sink_ref[kv]
            top = jnp.maximum(jnp.maximum(s[0], s[1]), s[2])
            m = jnp.where(left, jnp.max(top[:, :BLOCK], axis=-1, keepdims=True),
                          jnp.max(top[:, BLOCK:], axis=-1, keepdims=True))
            m = jnp.maximum(m, sink)
            acc = sum(jnp.dot(jnp.exp2(s[j] - m).astype(BF16), vcat[i + j][kv], preferred_element_type=F32)
                      for j in range(3))
            m_out = jnp.where(first2, m[:, :LANES], m[:, LANES:])
            sink_out = jnp.where(first2, sink[:, :LANES], sink[:, LANES:])
            o = acc[:, :LANES] / (acc[:, LANES:] + jnp.exp2(sink_out - m_out))
            outs += [o[:BLOCK], o[BLOCK:]]
        tiles.append(jnp.concatenate(outs, axis=1).astype(BF16))
    return jnp.concatenate(tiles, axis=0)


def _attn_outproj_body(x_ref, a_ref, c_ref, d_ref, w_ref, *rest, nstep):
    attn_refs, o_ref = rest[:-1], rest[-1]
    acc = x_ref[...]
    for k, m_ref in ((0, a_ref), (2, c_ref), (3, d_ref)):
        acc = acc + jnp.dot(m_ref[...], w_ref[k * D_GROUP:(k + 1) * D_GROUP, :], preferred_element_type=F32)
    b = _attn_tile(*attn_refs, pl.program_id(0) % nstep, nstep)
    o_ref[...] = acc + jnp.dot(b, w_ref[D_GROUP:2 * D_GROUP, :], preferred_element_type=F32)


def _attn_outproj(x2d, qkv, sink, out_a, out_c, out_d, w, layer, seq):
    T = x2d.shape[0]
    rows = ATTN_QBLOCKS * BLOCK
    nblk = seq // BLOCK
    nstep = seq // rows
    last_blk = T // BLOCK - 1
    tabs = _rope_tables(seq)
    sink_tab = jnp.repeat(jnp.repeat((sink.astype(F32) * LOG2E).reshape(N_KV_HEADS, 2, 2), BLOCK, axis=1),
                          BLOCK, axis=2)
    kcol, vcol = (OFF_K - OFF_Q) // LANES, (OFF_V - OFF_Q) // LANES

    prev = lambda i: jnp.maximum(i * ATTN_QBLOCKS - 1, 0)
    nxt = lambda i: jnp.minimum((i + 1) * ATTN_QBLOCKS, last_blk)

    def own(width, col=0):
        return pl.BlockSpec((rows, width), lambda i: (i, col))

    def edge(col, sel):
        return pl.BlockSpec((BLOCK, LANES), lambda i: (sel(i), col))

    def tab_edge(sel):
        return [pl.BlockSpec((BLOCK, LANES), lambda i: (sel(i) % nblk, 0))] * 3

    return pl.pallas_call(
        functools.partial(_attn_outproj_body, nstep=nstep),
        grid=(T // rows,),
        in_specs=[
            own(D_MODEL), own(D_GROUP), own(D_GROUP), own(D_GROUP),
            pl.BlockSpec((None, D_MODEL, D_MODEL), lambda i: (layer, 0, 0), pipeline_mode=pl.Buffered(1)),
            own(D_GROUP),
            edge(kcol, prev), own(LANES, kcol), edge(kcol, nxt),
            edge(vcol, prev), own(LANES, vcol), edge(vcol, nxt),
            [pl.BlockSpec((rows, LANES), lambda i: (i % nstep, 0))] * 3, tab_edge(prev), tab_edge(nxt),
            pl.BlockSpec((N_KV_HEADS, 2 * BLOCK, 2 * BLOCK), lambda i: (0, 0, 0)),
        ],
        out_specs=own(D_MODEL),
        out_shape=jax.ShapeDtypeStruct((T, D_MODEL), F32),
        compiler_params=_params(("parallel",)),
        name="attention_outproj",
    )(x2d, out_a, out_c, out_d, w, qkv, qkv, qkv, qkv, qkv, qkv, qkv, list(tabs), list(tabs), list(tabs), sink_tab)


def _is(x, v):
    return abs(x - v) < 1e-12


def _scale(a, s):
    if a is None or _is(s, 0.0):
        return None
    if _is(s, 1.0):
        return a
    if _is(s, -1.0):
        return -a
    return a * s


def _cmul_const(a, w):
    ar, ai = a
    return (_psub(_scale(ar, w.real), _scale(ai, w.imag)), _padd(_scale(ar, w.imag), _scale(ai, w.real)))


def _padd(x, y):
    if x is None:
        return y
    if y is None:
        return x
    return x + y


def _psub(x, y):
    if y is None:
        return x
    if x is None:
        return -y
    return x - y


def _fft_blocks(xs, sign, leaves_done=False):
    n = len(xs)
    if n == 1 or (n == 2 and leaves_done):
        return list(xs)
    ev = _fft_blocks(xs[0::2], sign, leaves_done)
    od = _fft_blocks(xs[1::2], sign, leaves_done)
    out = [None] * n
    for k in range(n // 2):
        t = _cmul_const(od[k], np.exp(sign * 2j * math.pi * k / n))
        out[k] = (_padd(ev[k][0], t[0]), _padd(ev[k][1], t[1]))
        out[k + n // 2] = (_psub(ev[k][0], t[0]), _psub(ev[k][1], t[1]))
    return out


def _dense(a, like):
    return jnp.zeros_like(like) if a is None else a


def _dft_stack():
    k = np.arange(DFT_N, dtype=np.float64)
    ang = 2.0 * math.pi * np.outer(k, k) / DFT_N
    c, s = np.cos(ang), np.sin(ang)
    return jnp.asarray(np.stack([np.concatenate([c, -s], axis=1), np.concatenate([c, s], axis=1)]), F32)


def _twiddle_table(n1f):
    f1 = np.arange(n1f, dtype=np.float64)[:, None]
    t2 = np.arange(DFT_N, dtype=np.float64)[None, :]
    ang = (-2.0 * math.pi / (n1f * DFT_N)) * f1 * t2
    rep = lambda a: jnp.asarray(np.repeat(a.reshape(-1, 1), LANES, axis=1), F32)
    return rep(np.cos(ang)), rep(np.sin(ang))


def _block_dft(f, qr, qi):
    qr, qi = qr.astype(BF16), qi.astype(BF16)
    rhs = jnp.concatenate([jnp.concatenate([qr, qi], axis=1), jnp.concatenate([-qi, qr], axis=1)], axis=0)
    d = jnp.dot(f, rhs, preferred_element_type=F32)
    return d[:, 0:LANES], d[:, LANES:]


def _forward_rows(xs, n1f):
    half = n1f // 2
    out = [None] * n1f
    if len(xs) == n1f:
        out[:half] = _fft_blocks(xs[0::2], -1.0)
        out[half:] = _fft_blocks(xs[1::2], -1.0)
        return out
    branches = (xs, [_cmul_const(x, np.exp(-2j * math.pi * t / n1f)) for t, x in enumerate(xs)])
    for b, xb in enumerate(branches):
        out[b:half:2] = _fft_blocks(xb[0::2], -1.0)
        out[half + b::2] = _fft_blocks(xb[1::2], -1.0)
    return out


def _pair_factors(n1f, pruned):
    f = np.arange(n1f // 2, dtype=np.float64)
    ang = -2.0 * math.pi * ((f // 2) / (n1f // 2) if pruned else f / n1f)
    return jnp.asarray(np.cos(ang), F32), jnp.asarray(np.sin(ang), F32)


MID_BLOCKS = 8
FILTER_MID_BLOCKS = 8


def _block_rows(f):
    start = f * DFT_N
    return pl.ds(start if isinstance(start, int) else pl.multiple_of(start, DFT_N), DFT_N)


def _store_rows(q_scr, blocks, r, like):
    for f, (pr, pi) in enumerate(blocks):
        rows = pl.ds(f * DFT_N + r, SUBLANES)
        q_scr[rows, 0:LANES] = _dense(pr, like)
        q_scr[rows, LANES:] = _dense(pi, like)


def _pair_spectra(fst, q_scr, twr_ref, twi_ref, cr_ref, ci_ref, f, half):
    lo, hi = _block_rows(f), _block_rows(f + half)
    er, ei = q_scr[lo, 0:LANES], q_scr[lo, LANES:]
    orr, oi = q_scr[hi, 0:LANES], q_scr[hi, LANES:]
    cr, ci = cr_ref[f], ci_ref[f]
    wr, wi = orr * cr - oi * ci, orr * ci + oi * cr
    out = []
    for rows, pr, pi in ((lo, er + wr, ei + wi), (hi, er - wr, ei - wi)):
        tr, ti = twr_ref[rows, :], twi_ref[rows, :]
        out.append((rows, _block_dft(fst[0], pr * tr - pi * ti, pr * ti + pi * tr)))
    return out


def _hconv_body(u_ref, g_ref, kr_ref, ki_ref, twr_ref, twi_ref, cr_ref, ci_ref, fst_ref, b_ref, o_ref, q_scr,
                *, seq):
    n1 = seq // DFT_N
    n1f = 2 * n1
    fst = fst_ref[...]

    def fwd(i, carry):
        r = pl.multiple_of(i * SUBLANES, SUBLANES)
        xs = [(u_ref[pl.ds(t * DFT_N + r, SUBLANES), :], u_ref[pl.ds(seq + t * DFT_N + r, SUBLANES), :])
              for t in range(n1)]
        _store_rows(q_scr, _forward_rows(xs, n1f), r, xs[0][0])
        return carry

    lax.fori_loop(0, DFT_N // SUBLANES, fwd, 0)

    def mid(i, carry):
        pairs = [_pair_spectra(fst, q_scr, twr_ref, twi_ref, cr_ref, ci_ref, i * (MID_BLOCKS // 2) + k, n1)
                 for k in range(MID_BLOCKS // 2)]
        s = []
        for pair in pairs:
            sp = []
            for r, (xr, xi) in pair:
                kr, ki = kr_ref[r, :].astype(F32), ki_ref[r, :].astype(F32)
                rr, ri = _block_dft(fst[1], xr * kr - xi * ki, xr * ki + xi * kr)
                tr, ti = twr_ref[r, :], twi_ref[r, :]
                sp.append((rr * tr + ri * ti, ri * tr - rr * ti))
            (lo, _), (hi, _) = pair
            s.append((lo, sp[0][0] + sp[1][0], sp[0][1] + sp[1][1]))
            s.append((hi, sp[0][0] - sp[1][0], sp[0][1] - sp[1][1]))
        for r, sr_, si_ in s:
            q_scr[r, 0:LANES] = sr_
            q_scr[r, LANES:] = si_
        return carry

    lax.fori_loop(0, n1f // MID_BLOCKS, mid, 0)

    bias = b_ref[...]

    def inv(i, carry):
        r = pl.multiple_of(i * SUBLANES, SUBLANES)
        ss = [(q_scr[pl.ds(f * DFT_N + r, SUBLANES), 0:LANES], q_scr[pl.ds(f * DFT_N + r, SUBLANES), LANES:])
              for f in range(n1f)]
        ev = _fft_blocks(ss[0::2], 1.0, leaves_done=True)
        od = _fft_blocks(ss[1::2], 1.0, leaves_done=True)
        for t in range(n1):
            o = _cmul_const(od[t], np.exp(2j * math.pi * t / n1f))
            for b, y in enumerate((ev[t][0] + o[0], ev[t][1] + o[1])):
                rows = pl.ds(b * seq + t * DFT_N + r, SUBLANES)
                o_ref[rows, :] = (g_ref[rows, :] * (y + bias * u_ref[rows, :])).astype(o_ref.dtype)
        return carry

    lax.fori_loop(0, DFT_N // SUBLANES, inv, 0)


def _hconv(u, ucol, g, gcol, spec_r, spec_i, order, tw, fst, bias, nb, seq, out_dtype):
    n1f = 2 * seq // DFT_N
    nc = D_GROUP // LANES
    once = pl.Buffered(1)
    return pl.pallas_call(
        functools.partial(_hconv_body, seq=seq),
        grid=(nc, nb // 2),
        in_specs=[
            pl.BlockSpec((2 * seq, LANES), lambda c, p: (p, ucol + c)),
            pl.BlockSpec((2 * seq, LANES), lambda c, p: (p, gcol + c)),
            pl.BlockSpec((None, n1f * DFT_N, LANES), lambda c, p: (order, 0, c)),
            pl.BlockSpec((None, n1f * DFT_N, LANES), lambda c, p: (order, 0, c)),
            pl.BlockSpec((n1f * DFT_N, LANES), lambda c, p: (0, 0), pipeline_mode=once),
            pl.BlockSpec((n1f * DFT_N, LANES), lambda c, p: (0, 0), pipeline_mode=once),
            pl.BlockSpec(memory_space=pltpu.SMEM), pl.BlockSpec(memory_space=pltpu.SMEM),
            pl.BlockSpec((2, DFT_N, 2 * DFT_N), lambda c, p: (0, 0, 0)),
            pl.BlockSpec((1, LANES), lambda c, p: (0, c)),
        ],
        out_specs=pl.BlockSpec((2 * seq, LANES), lambda c, p: (p, c)),
        out_shape=jax.ShapeDtypeStruct((nb * seq, D_GROUP), out_dtype),
        scratch_shapes=[pltpu.VMEM((n1f * DFT_N, 2 * LANES), F32)],
        compiler_params=_params(("arbitrary", "arbitrary")),
        name="hyena_long_conv",
    )(u, g, spec_r, spec_i, tw[0], tw[1], *_pair_factors(n1f, True), fst, bias[order][None, :])


def _hidden_body(f_ref, w1_ref, b1_ref, w2_ref, b2_ref, fr_ref, o_ref):
    hi = lax.Precision.HIGHEST
    fr = fr_ref[...]
    h = jnp.sin(fr * (jnp.dot(f_ref[...], w1_ref[...], precision=hi, preferred_element_type=F32) + b1_ref[...]))
    o_ref[...] = jnp.sin(fr * (jnp.dot(h, w2_ref[...], precision=hi, preferred_element_type=F32) + b2_ref[...]))


def _filter_features(seq):
    bands = np.linspace(1e-4, HY_BANDS - 1, HY_BANDS)[None, :]

    def feats(j):
        wpos = (2.0 * math.pi / seq) * j
        return np.concatenate([j / (seq - 1.0), np.cos(bands * wpos), -np.sin(bands * wpos)], axis=-1)

    j = np.arange(seq, dtype=np.float64)[:, None]
    return jnp.asarray(np.concatenate([feats(j), feats(seq - j)], axis=1), F32)


def _blockdiag2(w):
    z = jnp.zeros_like(w)
    return jnp.concatenate([jnp.concatenate([w, z], axis=1), jnp.concatenate([z, w], axis=1)], axis=0)


def _filter_hidden(feats, w1, b1, w2, b2, freq):
    seq = feats.shape[0]
    two = lambda v: jnp.concatenate([v, v])[None, :]
    tr = 512
    full = lambda a: pl.BlockSpec(a.shape, lambda i: (0, 0))
    args = (_blockdiag2(w1), two(b1), _blockdiag2(w2), two(b2), two(freq))
    return pl.pallas_call(
        _hidden_body,
        grid=(seq // tr,),
        in_specs=[pl.BlockSpec((tr, 2 * HY_EMB), lambda i: (i, 0))] + [full(a) for a in args],
        out_specs=pl.BlockSpec((tr, 2 * HY_HIDDEN), lambda i: (i, 0)),
        out_shape=jax.ShapeDtypeStruct((seq, 2 * HY_HIDDEN), F32),
        compiler_params=_params(("parallel",)),
        name="hyena_filter_hidden",
    )(feats, *args)


FILTER_ROWS = 512


def _split_bf16(a):
    hi = a.astype(BF16)
    return hi, (a - hi.astype(F32)).astype(BF16)


def _hfilter_body(h_ref, w3_ref, d_ref, twr_ref, twi_ref, cr_ref, ci_ref, fst_ref, or_ref, oi_ref, k_scr, q_scr,
                  *, seq):
    n1f = 2 * seq // DFT_N
    dec = jnp.abs(d_ref[...])
    w_hi, w_lo = _split_bf16(w3_ref[...])
    norm = jnp.zeros((1, LANES), F32)
    for r0 in range(0, seq, FILTER_ROWS):
        j = (lax.broadcasted_iota(jnp.int32, (FILTER_ROWS, 1), 0) + r0).astype(F32)
        h_hi, h_lo = _split_bf16(h_ref[r0:r0 + FILTER_ROWS, :])
        k2 = (jnp.dot(h_hi, w_hi, preferred_element_type=F32) + jnp.dot(h_hi, w_lo, preferred_element_type=F32)
              + jnp.dot(h_lo, w_hi, preferred_element_type=F32))
        kf = k2[:, :LANES] * jnp.exp(-(j / (seq - 1.0)) * dec[:, :LANES])
        kb = k2[:, LANES:] * jnp.exp(-((seq - j) / (seq - 1.0)) * dec[:, LANES:])
        kb = jnp.where(j > 0.0, kb, 0.0)
        norm = norm + jnp.sum(jnp.abs(kf), axis=0, keepdims=True) + jnp.sum(jnp.abs(kb), axis=0, keepdims=True)
        k_scr[r0:r0 + FILTER_ROWS, :] = kf
        k_scr[seq + r0:seq + r0 + FILTER_ROWS, :] = kb
    scale = (1.0 / (2 * seq)) / norm
    for r0 in range(0, 2 * seq, FILTER_ROWS):
        k_scr[r0:r0 + FILTER_ROWS, :] = k_scr[r0:r0 + FILTER_ROWS, :] * scale
    fst = fst_ref[...]

    def fwd(i, carry):
        r = pl.multiple_of(i * SUBLANES, SUBLANES)
        xs = [(k_scr[pl.ds(t1 * DFT_N + r, SUBLANES), :], None) for t1 in range(n1f)]
        _store_rows(q_scr, _forward_rows(xs, n1f), r, xs[0][0])
        return carry

    lax.fori_loop(0, DFT_N // SUBLANES, fwd, 0)

    def mid(i, carry):
        pairs = [_pair_spectra(fst, q_scr, twr_ref, twi_ref, cr_ref, ci_ref, i * (FILTER_MID_BLOCKS // 2) + k,
                               n1f // 2) for k in range(FILTER_MID_BLOCKS // 2)]
        for pair in pairs:
            for r, (xr, xi) in pair:
                or_ref[r, :] = xr.astype(or_ref.dtype)
                oi_ref[r, :] = xi.astype(oi_ref.dtype)
        return carry

    lax.fori_loop(0, n1f // FILTER_MID_BLOCKS, mid, 0)


def _by_block(a):
    nc = D_GROUP // LANES
    lead = a.shape[:-1]
    a = a.reshape(lead + (HY_ORDER, 2, nc, LANES))
    return jnp.swapaxes(a, -3, -2).reshape(lead + (HY_ORDER * nc * 2 * LANES,))


def _hfilter(hid, w3, decay, tw, fst, seq):
    n1f = 2 * seq // DFT_N
    nc = D_GROUP // LANES
    direction = jnp.arange(w3.shape[1]) // D_GROUP % 2
    w3 = jnp.concatenate([jnp.where(direction == 0, w3, 0.0), jnp.where(direction == 1, w3, 0.0)], axis=0)
    w3 = _by_block(w3)
    dec = _by_block(decay)[None, :]
    once = pl.Buffered(1)
    pair = lambda o, c: (0, nc * o + c)
    out_spec = pl.BlockSpec((None, n1f * DFT_N, LANES), lambda o, c: (o, 0, c))
    out_shape = jax.ShapeDtypeStruct((HY_ORDER, n1f * DFT_N, D_GROUP), BF16)
    return pl.pallas_call(
        functools.partial(_hfilter_body, seq=seq),
        grid=(HY_ORDER, nc),
        in_specs=[
            pl.BlockSpec((seq, 2 * HY_HIDDEN), lambda o, c: (0, 0)),
            pl.BlockSpec((2 * HY_HIDDEN, 2 * LANES), pair),
            pl.BlockSpec((1, 2 * LANES), pair),
            pl.BlockSpec((n1f * DFT_N, LANES), lambda o, c: (0, 0), pipeline_mode=once),
            pl.BlockSpec((n1f * DFT_N, LANES), lambda o, c: (0, 0), pipeline_mode=once),
            pl.BlockSpec(memory_space=pltpu.SMEM), pl.BlockSpec(memory_space=pltpu.SMEM),
            pl.BlockSpec((2, DFT_N, 2 * DFT_N), lambda o, c: (0, 0, 0)),
        ],
        out_specs=[out_spec, out_spec],
        out_shape=[out_shape, out_shape],
        scratch_shapes=[pltpu.VMEM((2 * seq, LANES), F32), pltpu.VMEM((n1f * DFT_N, 2 * LANES), F32)],
        compiler_params=_params(("arbitrary", "arbitrary")),
        name="hyena_filter_spectrum",
    )(hid, w3, dec, tw[0], tw[1], *_pair_factors(n1f, False), fst)


def _trunk(x, p, fst):
    nb, seq, _ = x.shape
    x2d = x.reshape(nb * seq, D_MODEL)
    tw = _twiddle_table(2 * seq // DFT_N)
    feats = _filter_features(seq)
    vcol = 0
    g1col = D_GROUP // LANES
    g2col = 2 * D_GROUP // LANES
    for l in range(DEPTH):
        qkv, out_a, out_c, hz = _inmix(x2d, p["mix_norm_g"][l][None, :], p["w_in"], p["pool_w"],
                                       p["pool_scale"][l][None, :], p["sconv_w"][l], p["hy_short_w"][l], l, seq)
        hid = _filter_hidden(feats, p["hy_w1"][l], p["hy_b1"][l], p["hy_w2"][l], p["hy_b2"][l], p["hy_freq"][l])
        spec_r, spec_i = _hfilter(hid, p["hy_w3"][l], p["hy_decay"][l], tw, fst, seq)
        bias = p["hy_bias"][l]
        u1 = _hconv(hz, vcol, hz, g1col, spec_r, spec_i, 0, tw, fst, bias, nb, seq, F32)
        out_d = _hconv(u1, 0, hz, g2col, spec_r, spec_i, 1, tw, fst, bias, nb, seq, BF16)
        x2d = _attn_outproj(x2d, qkv, p["attn_sink"][l], out_a, out_c, out_d, p["w_out"], l, seq)
        x2d = _ffn(x2d, p["ffn_norm_g"][l][None, :], p["w_gate_up"], p["w_down"],
                   p["final_norm_g"][None, :], l, final_norm=(l == DEPTH - 1))
    return x2d.reshape(nb, seq, D_MODEL)


def kernel(x_prompt, x_sample, mix_norm_g, w_in, pool_w, pool_scale, attn_sink, sconv_w, hy_short_w, hy_w1, hy_b1, hy_w2, hy_b2, hy_w3, hy_freq, hy_decay, hy_bias, w_out, ffn_norm_g, w_gate_up, w_down, final_norm_g):
    p = dict(
        mix_norm_g=mix_norm_g, w_in=w_in.astype(BF16), pool_w=pool_w.astype(BF16), pool_scale=pool_scale,
        attn_sink=attn_sink, sconv_w=sconv_w, hy_short_w=hy_short_w, hy_w1=hy_w1, hy_b1=hy_b1, hy_w2=hy_w2,
        hy_b2=hy_b2, hy_w3=hy_w3, hy_freq=hy_freq, hy_decay=hy_decay, hy_bias=hy_bias,
        w_out=w_out.astype(BF16), ffn_norm_g=ffn_norm_g, w_gate_up=w_gate_up,
        w_down=w_down, final_norm_g=final_norm_g,
    )
    fst = _dft_stack().astype(BF16)
    return (_trunk(x_prompt, p, fst), _trunk(x_sample, p, fst))
```

```python
import functools
import math

import numpy as np
import jax
import jax.numpy as jnp
from jax import lax
from jax.experimental import pallas as pl
from jax.experimental.pallas import tpu as pltpu

F32 = jnp.float32
BF16 = jnp.bfloat16

D_MODEL = 2048
DEPTH = 2
D_GROUP = 512
POOL_WINDOWS = (2, 4, 8, 16)
POOL_CH = 128
HEAD_DIM = 64
N_Q_HEADS = 8
N_KV_HEADS = 2
GQA_GROUP = 4
BLOCK = 128
ROPE_DIM = 16
ROPE_THETA = 500000.0
HY_ORDER = 2
HY_EMB = 33
HY_BANDS = 16
HY_HIDDEN = 64
D_FF = 5632
NORM_EPS = 1e-6
OFF_Q = 512
OFF_K = 1024
OFF_V = 1152
OFF_C = 1280
OFF_D = 2816
D_IN = 4352

LANES = 128
SUBLANES = 8
DFT_N = 256
VMEM_LIMIT = 56 * 2 ** 20
FFN_VMEM_LIMIT = 60 * 2 ** 20


def _params(sem, vmem=VMEM_LIMIT):
    return pltpu.CompilerParams(dimension_semantics=sem, vmem_limit_bytes=vmem)


def _rms(x, g):
    ms = jnp.mean(x * x, axis=-1, keepdims=True)
    return x * lax.rsqrt(ms + NORM_EPS) * g


HALO = SUBLANES
D_QKV = OFF_C - OFF_Q


def _conv3_ext(z, w_ref):
    n = z.shape[0]
    return (pltpu.roll(z, 1, axis=0) * w_ref[0:1, :] + z * w_ref[1:2, :]
            + pltpu.roll(z, n - 1, axis=0) * w_ref[2:3, :])


def _inmix_body(x_ref, xp_ref, xn_ref, g_ref, w_ref, pw_ref, ps_ref, cw_ref, hw_ref,
                qkv_ref, a_ref, c_ref, hz_ref, *, seq):
    tm = x_ref.shape[0]
    ext = tm + 2 * HALO
    core = slice(HALO, HALO + tm)
    g = g_ref[...]
    pos0 = (pl.program_id(0) * tm) % seq
    before = jnp.where(pos0 > 0, _rms(xp_ref[...], g), 0.0)
    after = jnp.where(pos0 + tm < seq, _rms(xn_ref[...], g), 0.0)
    hn = jnp.concatenate([before, _rms(x_ref[...], g), after], axis=0).astype(BF16)
    row = lax.broadcasted_iota(jnp.int32, (ext, 1), 0)

    def project(lo, hi):
        return jnp.dot(hn, w_ref[:, lo:hi], preferred_element_type=F32)

    qkv_ref[...] = project(OFF_Q, OFF_C)[core]

    za = project(0, OFF_Q)
    pos = pos0 + row - HALO
    for k, win in enumerate(POOL_WINDOWS):
        cols = slice(k * POOL_CH, (k + 1) * POOL_CH)
        p = za[:, cols]
        s = p + pltpu.roll(p, 1, axis=0)
        step = 1
        while 2 * step < win:
            s = pltpu.roll(s, step, axis=0) + pltpu.roll(s, ext - step, axis=0)
            step *= 2
        half = win // 2
        cnt = jnp.maximum(jnp.minimum(pos + half, seq) - jnp.maximum(pos - half, 0), 1).astype(F32)
        d = (s / cnt - p)[core]
        out = jnp.dot(d.astype(BF16), pw_ref[k], preferred_element_type=F32) * ps_ref[:, cols]
        a_ref[:, cols] = out.astype(a_ref.dtype)

    zc = project(OFF_C, OFF_D)
    ch, cb, cc = zc[:, :D_GROUP], zc[:, D_GROUP:2 * D_GROUP], zc[:, 2 * D_GROUP:]
    c_ref[...] = (cb * _conv3_ext(cc * ch, cw_ref))[core].astype(c_ref.dtype)

    hz_ref[...] = _conv3_ext(project(OFF_D, D_IN), hw_ref)[core]


def _inmix(x2d, g, w, pool_w, pool_scale, sconv_w, hy_short_w, layer, seq):
    T = x2d.shape[0]
    tm = 512
    per = tm // HALO
    last = T // HALO - 1
    full = lambda a: pl.BlockSpec(a.shape, lambda i: (0, 0))
    row_tile = lambda width: pl.BlockSpec((tm, width), lambda i: (i, 0))
    return pl.pallas_call(
        functools.partial(_inmix_body, seq=seq),
        grid=(T // tm,),
        in_specs=[
            row_tile(D_MODEL),
            pl.BlockSpec((HALO, D_MODEL), lambda i: (jnp.maximum(i * per - 1, 0), 0)),
            pl.BlockSpec((HALO, D_MODEL), lambda i: (jnp.minimum((i + 1) * per, last), 0)),
            full(g),
            pl.BlockSpec((None, D_MODEL, D_IN), lambda i: (layer, 0, 0), pipeline_mode=pl.Buffered(1)),
            pl.BlockSpec((None, len(POOL_WINDOWS), POOL_CH, POOL_CH), lambda i: (layer, 0, 0, 0)),
            full(pool_scale), full(sconv_w), full(hy_short_w),
        ],
        out_specs=[row_tile(D_QKV), row_tile(D_GROUP), row_tile(D_GROUP), row_tile(3 * D_GROUP)],
        out_shape=[jax.ShapeDtypeStruct((T, D_QKV), F32), jax.ShapeDtypeStruct((T, D_GROUP), BF16),
                   jax.ShapeDtypeStruct((T, D_GROUP), BF16), jax.ShapeDtypeStruct((T, 3 * D_GROUP), F32)],
        compiler_params=_params(("parallel",)),
        name="inproj_local_mixers",
    )(x2d, x2d, x2d, g, w, pool_w, pool_scale, sconv_w, hy_short_w)


def _ffn_body(x_hbm, g_ref, wg_ref, wu_ref, wd_ref, gf_ref, o_ref, h_ref, x_scr, sem, *, final_norm):
    i, j = pl.program_id(0), pl.program_id(1)
    tm = x_scr.shape[0]

    def x_copy(tile):
        return pltpu.make_async_copy(x_hbm.at[pl.ds(pl.multiple_of(tile * tm, tm), tm), :], x_scr, sem)

    @pl.when((i == 0) & (j == 0))
    def _():
        x_copy(0).start()

    @pl.when(j == 0)
    def _():
        x_copy(i).wait()
        x = x_scr[...]
        h_ref[...] = _rms(x, g_ref[...]).astype(BF16)
        o_ref[...] = x

        @pl.when(i + 1 < pl.num_programs(0))
        def _():
            x_copy(i + 1).start()

    h = h_ref[...]
    gate = jnp.dot(h, wg_ref[...].astype(BF16), preferred_element_type=F32)
    up = jnp.dot(h, wu_ref[...].astype(BF16), preferred_element_type=F32)
    act = (gate * jax.nn.sigmoid(gate) * up).astype(BF16)
    o_ref[...] += jnp.dot(act, wd_ref[...].astype(BF16), preferred_element_type=F32)

    if final_norm:
        @pl.when(j == pl.num_programs(1) - 1)
        def _():
            o_ref[...] = _rms(o_ref[...], gf_ref[...])


def _ffn(x2d, g, w_gate_up, w_down, gf, layer, final_norm):
    T = x2d.shape[0]
    tm, tf = 1024, 512
    nf = D_FF // tf
    return pl.pallas_call(
        functools.partial(_ffn_body, final_norm=final_norm),
        grid=(T // tm, nf),
        in_specs=[
            pl.BlockSpec(memory_space=pl.ANY),
            pl.BlockSpec((1, D_MODEL), lambda i, j: (0, 0)),
            pl.BlockSpec((None, D_MODEL, tf), lambda i, j: (layer, 0, j)),
            pl.BlockSpec((None, D_MODEL, tf), lambda i, j: (layer, 0, j + nf)),
            pl.BlockSpec((None, tf, D_MODEL), lambda i, j: (layer, j, 0)),
            pl.BlockSpec((1, D_MODEL), lambda i, j: (0, 0)),
        ],
        out_specs=pl.BlockSpec((tm, D_MODEL), lambda i, j: (i, 0)),
        out_shape=jax.ShapeDtypeStruct((T, D_MODEL), F32),
        scratch_shapes=[pltpu.VMEM((tm, D_MODEL), BF16), pltpu.VMEM((tm, D_MODEL), F32),
                        pltpu.SemaphoreType.DMA(())],
        compiler_params=_params(("arbitrary", "arbitrary"), FFN_VMEM_LIMIT),
        name="ffn",
    )(x2d, g, w_gate_up, w_gate_up, w_down, gf)


def _rope_tables(seq):
    inv_freq = ROPE_THETA ** (-np.arange(0, ROPE_DIM, 2, dtype=np.float64) / ROPE_DIM)
    ang = np.arange(seq, dtype=np.float64)[:, None] * inv_freq[None, :]
    half = ROPE_DIM // 2
    cos = np.ones((seq, HEAD_DIM))
    s_lo = np.zeros((seq, HEAD_DIM))
    s_hi = np.zeros((seq, HEAD_DIM))
    cos[:, :half] = np.cos(ang)
    cos[:, half:ROPE_DIM] = np.cos(ang)
    s_lo[:, half:ROPE_DIM] = np.sin(ang)
    s_hi[:, :half] = -np.sin(ang)
    rep = LANES // HEAD_DIM
    return tuple(jnp.asarray(np.tile(a, (1, rep)), F32) for a in (cos, s_lo, s_hi))


LOG2E = math.log2(math.e)
ATTN_QBLOCKS = 4


def _rope(x, cos, s_lo, s_hi):
    half = ROPE_DIM // 2
    return x * cos + pltpu.roll(x, half, axis=1) * s_lo + pltpu.roll(x, LANES - half, axis=1) * s_hi


def _attn_tile(q_ref, kp_ref, kc_ref, kn_ref, vp_ref, vc_ref, vn_ref, cq_ref, cp_ref, cn_ref, sink_ref, n, nstep):
    two = 2 * BLOCK
    blk_rows = lambda i: slice(i * BLOCK, (i + 1) * BLOCK)
    tq = tuple(r[...] for r in cq_ref)

    q = q_ref[...]
    qscale = HEAD_DIM ** -0.5 * LOG2E
    qg = [(_rope(q[:, g * LANES:(g + 1) * LANES], *tq) * qscale).astype(BF16) for g in range(D_GROUP // LANES)]

    lane = lax.broadcasted_iota(jnp.int32, (BLOCK, LANES), 1)
    first = lane < HEAD_DIM
    half_ones = jnp.where(first, 1.0, 0.0)
    ones_bd = jnp.concatenate([half_ones, 1.0 - half_ones], axis=0)

    def per_head(x):
        xr = pltpu.roll(x, HEAD_DIM, axis=1)
        return [jnp.concatenate([jnp.where(first, x, 0.0), jnp.where(first, 0.0, xr)], axis=0),
                jnp.concatenate([jnp.where(first, xr, 0.0), jnp.where(first, 0.0, x)], axis=0)]

    kc = _rope(kc_ref[...], *tq)
    vc = vc_ref[...]
    kblocks = ([_rope(kp_ref[...], *(r[...] for r in cp_ref))] + [kc[blk_rows(i)] for i in range(ATTN_QBLOCKS)]
               + [_rope(kn_ref[...], *(r[...] for r in cn_ref))])
    vblocks = [vp_ref[...]] + [vc[blk_rows(i)] for i in range(ATTN_QBLOCKS)] + [vn_ref[...]]
    kcat = [[a.astype(BF16) for a in per_head(k)] for k in kblocks]
    vcat = [[jnp.concatenate([a, ones_bd], axis=1).astype(BF16) for a in per_head(v)] for v in vblocks]

    r = lax.broadcasted_iota(jnp.int32, (two, two), 0) % BLOCK
    c = lax.broadcasted_iota(jnp.int32, (two, two), 1)
    left = c < BLOCK
    c = c % BLOCK
    below, above = c >= r, c <= r
    first2 = lax.broadcasted_iota(jnp.int32, (two, LANES), 1) < HEAD_DIM

    tiles = []
    for i in range(ATTN_QBLOCKS):
        lo_mask = (below & (n > 0)) if i == 0 else below
        hi_mask = (above & (n < nstep - 1)) if i == ATTN_QBLOCKS - 1 else above
        masks = [lo_mask, None, hi_mask]
        outs = []
        for kv in range(N_KV_HEADS):
            qs = jnp.concatenate([qg[2 * kv][blk_rows(i)], qg[2 * kv + 1][blk_rows(i)]], axis=0)
            s = []
            for j in range(3):
                sb = lax.dot_general(qs, kcat[i + j][kv], (((1,), (1,)), ((), ())), preferred_element_type=F32)
                if masks[j] is not None:
                    sb = jnp.where(masks[j], sb, -jnp.inf)
                s.append(sb)
            sink = sink_ref[kv]
            top = jnp.maximum(jnp.maximum(s[0], s[1]), s[2])
            m = jnp.where(left, jnp.max(top[:, :BLOCK], axis=-1, keepdims=True),
                          jnp.max(top[:, BLOCK:], axis=-1, keepdims=True))
            m = jnp.maximum(m, sink)
            acc = sum(jnp.dot(jnp.exp2(s[j] - m).astype(BF16), vcat[i + j][kv], preferred_element_type=F32)
                      for j in range(3))
            m_out = jnp.where(first2, m[:, :LANES], m[:, LANES:])
            sink_out = jnp.where(first2, sink[:, :LANES], sink[:, LANES:])
            o = acc[:, :LANES] / (acc[:, LANES:] + jnp.exp2(sink_out - m_out))
            outs += [o[:BLOCK], o[BLOCK:]]
        tiles.append(jnp.concatenate(outs, axis=1).astype(BF16))
    return jnp.concatenate(tiles, axis=0)


def _attn_outproj_body(x_ref, a_ref, c_ref, d_ref, w_ref, *rest, nstep):
    attn_refs, o_ref = rest[:-1], rest[-1]
    mixer_rows = lambda k: w_ref[k * D_GROUP:(k + 1) * D_GROUP, :].astype(BF16)
    acc = x_ref[...]
    for k, m_ref in ((0, a_ref), (2, c_ref), (3, d_ref)):
        acc = acc + jnp.dot(m_ref[...], mixer_rows(k), preferred_element_type=F32)
    b = _attn_tile(*attn_refs, pl.program_id(0) % nstep, nstep)
    o_ref[...] = acc + jnp.dot(b, mixer_rows(1), preferred_element_type=F32)


def _attn_outproj(x2d, qkv, sink, out_a, out_c, out_d, w, layer, seq):
    T = x2d.shape[0]
    rows = ATTN_QBLOCKS * BLOCK
    nblk = seq // BLOCK
    nstep = seq // rows
    last_blk = T // BLOCK - 1
    tabs = _rope_tables(seq)
    sink_tab = jnp.repeat(jnp.repeat((sink.astype(F32) * LOG2E).reshape(N_KV_HEADS, 2, 2), BLOCK, axis=1),
                          BLOCK, axis=2)
    kcol, vcol = (OFF_K - OFF_Q) // LANES, (OFF_V - OFF_Q) // LANES

    prev = lambda i: jnp.maximum(i * ATTN_QBLOCKS - 1, 0)
    nxt = lambda i: jnp.minimum((i + 1) * ATTN_QBLOCKS, last_blk)

    def own(width, col=0):
        return pl.BlockSpec((rows, width), lambda i: (i, col))

    def edge(col, sel):
        return pl.BlockSpec((BLOCK, LANES), lambda i: (sel(i), col))

    def tab_edge(sel):
        return [pl.BlockSpec((BLOCK, LANES), lambda i: (sel(i) % nblk, 0))] * 3

    return pl.pallas_call(
        functools.partial(_attn_outproj_body, nstep=nstep),
        grid=(T // rows,),
        in_specs=[
            own(D_MODEL), own(D_GROUP), own(D_GROUP), own(D_GROUP),
            pl.BlockSpec((None, D_MODEL, D_MODEL), lambda i: (layer, 0, 0), pipeline_mode=pl.Buffered(1)),
            own(D_GROUP),
            edge(kcol, prev), own(LANES, kcol), edge(kcol, nxt),
            edge(vcol, prev), own(LANES, vcol), edge(vcol, nxt),
            [pl.BlockSpec((rows, LANES), lambda i: (i % nstep, 0))] * 3, tab_edge(prev), tab_edge(nxt),
            pl.BlockSpec((N_KV_HEADS, 2 * BLOCK, 2 * BLOCK), lambda i: (0, 0, 0)),
        ],
        out_specs=own(D_MODEL),
        out_shape=jax.ShapeDtypeStruct((T, D_MODEL), F32),
        compiler_params=_params(("parallel",)),
        name="attention_outproj",
    )(x2d, out_a, out_c, out_d, w, qkv, qkv, qkv, qkv, qkv, qkv, qkv, list(tabs), list(tabs), list(tabs), sink_tab)


def _is(x, v):
    return abs(x - v) < 1e-12


def _scale(a, s):
    if a is None or _is(s, 0.0):
        return None
    if _is(s, 1.0):
        return a
    if _is(s, -1.0):
        return -a
    return a * s


def _cmul_const(a, w):
    ar, ai = a
    return (_psub(_scale(ar, w.real), _scale(ai, w.imag)), _padd(_scale(ar, w.imag), _scale(ai, w.real)))


def _padd(x, y):
    if x is None:
        return y
    if y is None:
        return x
    return x + y


def _psub(x, y):
    if y is None:
        return x
    if x is None:
        return -y
    return x - y


def _fft_blocks(xs, sign, leaves_done=False):
    n = len(xs)
    if n == 1 or (n == 2 and leaves_done):
        return list(xs)
    ev = _fft_blocks(xs[0::2], sign, leaves_done)
    od = _fft_blocks(xs[1::2], sign, leaves_done)
    out = [None] * n
    for k in range(n // 2):
        t = _cmul_const(od[k], np.exp(sign * 2j * math.pi * k / n))
        out[k] = (_padd(ev[k][0], t[0]), _padd(ev[k][1], t[1]))
        out[k + n // 2] = (_psub(ev[k][0], t[0]), _psub(ev[k][1], t[1]))
    return out


def _dense(a, like):
    return jnp.zeros_like(like) if a is None else a


def _dft_stack():
    k = np.arange(DFT_N, dtype=np.float64)
    ang = 2.0 * math.pi * np.outer(k, k) / DFT_N
    c, s = np.cos(ang), np.sin(ang)
    return jnp.asarray(np.stack([np.concatenate([c, -s], axis=1), np.concatenate([c, s], axis=1)]), F32)


def _twiddle_table(n1f):
    f1 = np.arange(n1f, dtype=np.float64)[:, None]
    t2 = np.arange(DFT_N, dtype=np.float64)[None, :]
    ang = (-2.0 * math.pi / (n1f * DFT_N)) * f1 * t2
    rep = lambda a: jnp.asarray(np.repeat(a.reshape(-1, 1), LANES, axis=1), F32)
    return rep(np.cos(ang)), rep(np.sin(ang))


def _block_dft(f, qr, qi):
    qr, qi = qr.astype(BF16), qi.astype(BF16)
    rhs = jnp.concatenate([jnp.concatenate([qr, qi], axis=1), jnp.concatenate([-qi, qr], axis=1)], axis=0)
    d = jnp.dot(f, rhs, preferred_element_type=F32)
    return d[:, 0:LANES], d[:, LANES:]


def _forward_rows(xs, n1f):
    half = n1f // 2
    out = [None] * n1f
    if len(xs) == n1f:
        out[:half] = _fft_blocks(xs[0::2], -1.0)
        out[half:] = _fft_blocks(xs[1::2], -1.0)
        return out
    branches = (xs, [_cmul_const(x, np.exp(-2j * math.pi * t / n1f)) for t, x in enumerate(xs)])
    for b, xb in enumerate(branches):
        out[b:half:2] = _fft_blocks(xb[0::2], -1.0)
        out[half + b::2] = _fft_blocks(xb[1::2], -1.0)
    return out


def _pair_factors(n1f, pruned):
    f = np.arange(n1f // 2, dtype=np.float64)
    ang = -2.0 * math.pi * ((f // 2) / (n1f // 2) if pruned else f / n1f)
    return jnp.asarray(np.cos(ang), F32), jnp.asarray(np.sin(ang), F32)


MID_BLOCKS = 8
FILTER_MID_BLOCKS = 8


def _block_rows(f):
    start = f * DFT_N
    return pl.ds(start if isinstance(start, int) else pl.multiple_of(start, DFT_N), DFT_N)


def _store_rows(q_scr, blocks, r, like):
    for f, (pr, pi) in enumerate(blocks):
        rows = pl.ds(f * DFT_N + r, SUBLANES)
        q_scr[rows, 0:LANES] = _dense(pr, like)
        q_scr[rows, LANES:] = _dense(pi, like)


def _pair_spectra(fst, q_scr, twr_ref, twi_ref, cr_ref, ci_ref, f, half):
    lo, hi = _block_rows(f), _block_rows(f + half)
    er, ei = q_scr[lo, 0:LANES], q_scr[lo, LANES:]
    orr, oi = q_scr[hi, 0:LANES], q_scr[hi, LANES:]
    cr, ci = cr_ref[f], ci_ref[f]
    wr, wi = orr * cr - oi * ci, orr * ci + oi * cr
    out = []
    for rows, pr, pi in ((lo, er + wr, ei + wi), (hi, er - wr, ei - wi)):
        tr, ti = twr_ref[rows, :], twi_ref[rows, :]
        out.append((rows, _block_dft(fst[0], pr * tr - pi * ti, pr * ti + pi * tr)))
    return out


def _hconv_body(u_ref, g_ref, kr_ref, ki_ref, twr_ref, twi_ref, cr_ref, ci_ref, fst_ref, b_ref, o_ref, q_scr,
                *, seq):
    n1 = seq // DFT_N
    n1f = 2 * n1
    fst = fst_ref[...]

    def fwd(i, carry):
        r = pl.multiple_of(i * SUBLANES, SUBLANES)
        xs = [(u_ref[pl.ds(t * DFT_N + r, SUBLANES), :], u_ref[pl.ds(seq + t * DFT_N + r, SUBLANES), :])
              for t in range(n1)]
        _store_rows(q_scr, _forward_rows(xs, n1f), r, xs[0][0])
        return carry

    lax.fori_loop(0, DFT_N // SUBLANES, fwd, 0)

    def mid(i, carry):
        pairs = [_pair_spectra(fst, q_scr, twr_ref, twi_ref, cr_ref, ci_ref, i * (MID_BLOCKS // 2) + k, n1)
                 for k in range(MID_BLOCKS // 2)]
        s = []
        for pair in pairs:
            sp = []
            for r, (xr, xi) in pair:
                kr, ki = kr_ref[r, :].astype(F32), ki_ref[r, :].astype(F32)
                rr, ri = _block_dft(fst[1], xr * kr - xi * ki, xr * ki + xi * kr)
                tr, ti = twr_ref[r, :], twi_ref[r, :]
                sp.append((rr * tr + ri * ti, ri * tr - rr * ti))
            (lo, _), (hi, _) = pair
            s.append((lo, sp[0][0] + sp[1][0], sp[0][1] + sp[1][1]))
            s.append((hi, sp[0][0] - sp[1][0], sp[0][1] - sp[1][1]))
        for r, sr_, si_ in s:
            q_scr[r, 0:LANES] = sr_
            q_scr[r, LANES:] = si_
        return carry

    lax.fori_loop(0, n1f // MID_BLOCKS, mid, 0)

    bias = b_ref[...]

    def inv(i, carry):
        r = pl.multiple_of(i * SUBLANES, SUBLANES)
        ss = [(q_scr[pl.ds(f * DFT_N + r, SUBLANES), 0:LANES], q_scr[pl.ds(f * DFT_N + r, SUBLANES), LANES:])
              for f in range(n1f)]
        ev = _fft_blocks(ss[0::2], 1.0, leaves_done=True)
        od = _fft_blocks(ss[1::2], 1.0, leaves_done=True)
        for t in range(n1):
            o = _cmul_const(od[t], np.exp(2j * math.pi * t / n1f))
            for b, y in enumerate((ev[t][0] + o[0], ev[t][1] + o[1])):
                rows = pl.ds(b * seq + t * DFT_N + r, SUBLANES)
                o_ref[rows, :] = (g_ref[rows, :] * (y + bias * u_ref[rows, :])).astype(o_ref.dtype)
        return carry

    lax.fori_loop(0, DFT_N // SUBLANES, inv, 0)


def _hconv(u, ucol, g, gcol, spec_r, spec_i, order, tw, fst, bias, nb, seq, out_dtype):
    n1f = 2 * seq // DFT_N
    nc = D_GROUP // LANES
    once = pl.Buffered(1)
    return pl.pallas_call(
        functools.partial(_hconv_body, seq=seq),
        grid=(nc, nb // 2),
        in_specs=[
            pl.BlockSpec((2 * seq, LANES), lambda c, p: (p, ucol + c)),
            pl.BlockSpec((2 * seq, LANES), lambda c, p: (p, gcol + c)),
            pl.BlockSpec((None, n1f * DFT_N, LANES), lambda c, p: (order, 0, c)),
            pl.BlockSpec((None, n1f * DFT_N, LANES), lambda c, p: (order, 0, c)),
            pl.BlockSpec((n1f * DFT_N, LANES), lambda c, p: (0, 0), pipeline_mode=once),
            pl.BlockSpec((n1f * DFT_N, LANES), lambda c, p: (0, 0), pipeline_mode=once),
            pl.BlockSpec(memory_space=pltpu.SMEM), pl.BlockSpec(memory_space=pltpu.SMEM),
            pl.BlockSpec((2, DFT_N, 2 * DFT_N), lambda c, p: (0, 0, 0)),
            pl.BlockSpec((1, LANES), lambda c, p: (0, c)),
        ],
        out_specs=pl.BlockSpec((2 * seq, LANES), lambda c, p: (p, c)),
        out_shape=jax.ShapeDtypeStruct((nb * seq, D_GROUP), out_dtype),
        scratch_shapes=[pltpu.VMEM((n1f * DFT_N, 2 * LANES), F32)],
        compiler_params=_params(("arbitrary", "arbitrary")),
        name="hyena_long_conv",
    )(u, g, spec_r, spec_i, tw[0], tw[1], *_pair_factors(n1f, True), fst, bias[order][None, :])


def _hidden_body(f_ref, w1_ref, b1_ref, w2_ref, b2_ref, fr_ref, o_ref):
    hi = lax.Precision.HIGHEST
    fr = fr_ref[...]
    h = jnp.sin(fr * (jnp.dot(f_ref[...], w1_ref[...], precision=hi, preferred_element_type=F32) + b1_ref[...]))
    o_ref[...] = jnp.sin(fr * (jnp.dot(h, w2_ref[...], precision=hi, preferred_element_type=F32) + b2_ref[...]))


def _filter_features(seq):
    bands = np.linspace(1e-4, HY_BANDS - 1, HY_BANDS)[None, :]

    def feats(j):
        wpos = (2.0 * math.pi / seq) * j
        return np.concatenate([j / (seq - 1.0), np.cos(bands * wpos), -np.sin(bands * wpos)], axis=-1)

    j = np.arange(seq, dtype=np.float64)[:, None]
    return jnp.asarray(np.concatenate([feats(j), feats(seq - j)], axis=1), F32)


def _blockdiag2(w):
    z = jnp.zeros_like(w)
    return jnp.concatenate([jnp.concatenate([w, z], axis=1), jnp.concatenate([z, w], axis=1)], axis=0)


def _filter_hidden(feats, w1, b1, w2, b2, freq):
    seq = feats.shape[0]
    two = lambda v: jnp.concatenate([v, v])[None, :]
    tr = 512
    full = lambda a: pl.BlockSpec(a.shape, lambda i: (0, 0))
    args = (_blockdiag2(w1), two(b1), _blockdiag2(w2), two(b2), two(freq))
    return pl.pallas_call(
        _hidden_body,
        grid=(seq // tr,),
        in_specs=[pl.BlockSpec((tr, 2 * HY_EMB), lambda i: (i, 0))] + [full(a) for a in args],
        out_specs=pl.BlockSpec((tr, 2 * HY_HIDDEN), lambda i: (i, 0)),
        out_shape=jax.ShapeDtypeStruct((seq, 2 * HY_HIDDEN), F32),
        compiler_params=_params(("parallel",)),
        name="hyena_filter_hidden",
    )(feats, *args)


FILTER_ROWS = 512


def _split_bf16(a):
    hi = a.astype(BF16)
    return hi, (a - hi.astype(F32)).astype(BF16)


def _hfilter_body(h_ref, w3_ref, d_ref, twr_ref, twi_ref, cr_ref, ci_ref, fst_ref, or_ref, oi_ref, k_scr, q_scr,
                  *, seq):
    n1f = 2 * seq // DFT_N
    dec = jnp.abs(d_ref[...])
    w_hi, w_lo = _split_bf16(w3_ref[...])
    norm = jnp.zeros((1, LANES), F32)
    for r0 in range(0, seq, FILTER_ROWS):
        j = (lax.broadcasted_iota(jnp.int32, (FILTER_ROWS, 1), 0) + r0).astype(F32)
        h_hi, h_lo = _split_bf16(h_ref[r0:r0 + FILTER_ROWS, :])
        k2 = (jnp.dot(h_hi, w_hi, preferred_element_type=F32) + jnp.dot(h_hi, w_lo, preferred_element_type=F32)
              + jnp.dot(h_lo, w_hi, preferred_element_type=F32))
        kf = k2[:, :LANES] * jnp.exp(-(j / (seq - 1.0)) * dec[:, :LANES])
        kb = k2[:, LANES:] * jnp.exp(-((seq - j) / (seq - 1.0)) * dec[:, LANES:])
        kb = jnp.where(j > 0.0, kb, 0.0)
        norm = norm + jnp.sum(jnp.abs(kf), axis=0, keepdims=True) + jnp.sum(jnp.abs(kb), axis=0, keepdims=True)
        k_scr[r0:r0 + FILTER_ROWS, :] = kf
        k_scr[seq + r0:seq + r0 + FILTER_ROWS, :] = kb
    scale = (1.0 / (2 * seq)) / norm
    for r0 in range(0, 2 * seq, FILTER_ROWS):
        k_scr[r0:r0 + FILTER_ROWS, :] = k_scr[r0:r0 + FILTER_ROWS, :] * scale
    fst = fst_ref[...]

    def fwd(i, carry):
        r = pl.multiple_of(i * SUBLANES, SUBLANES)
        xs = [(k_scr[pl.ds(t1 * DFT_N + r, SUBLANES), :], None) for t1 in range(n1f)]
        _store_rows(q_scr, _forward_rows(xs, n1f), r, xs[0][0])
        return carry

    lax.fori_loop(0, DFT_N // SUBLANES, fwd, 0)

    def mid(i, carry):
        pairs = [_pair_spectra(fst, q_scr, twr_ref, twi_ref, cr_ref, ci_ref, i * (FILTER_MID_BLOCKS // 2) + k,
                               n1f // 2) for k in range(FILTER_MID_BLOCKS // 2)]
        for pair in pairs:
            for r, (xr, xi) in pair:
                or_ref[r, :] = xr.astype(or_ref.dtype)
                oi_ref[r, :] = xi.astype(oi_ref.dtype)
        return carry

    lax.fori_loop(0, n1f // FILTER_MID_BLOCKS, mid, 0)


def _by_block(a):
    nc = D_GROUP // LANES
    lead = a.shape[:-1]
    a = a.reshape(lead + (HY_ORDER, 2, nc, LANES))
    return jnp.swapaxes(a, -3, -2).reshape(lead + (HY_ORDER * nc * 2 * LANES,))


def _hfilter(hid, w3, decay, tw, fst, seq):
    n1f = 2 * seq // DFT_N
    nc = D_GROUP // LANES
    direction = jnp.arange(w3.shape[1]) // D_GROUP % 2
    w3 = jnp.concatenate([jnp.where(direction == 0, w3, 0.0), jnp.where(direction == 1, w3, 0.0)], axis=0)
    w3 = _by_block(w3)
    dec = _by_block(decay)[None, :]
    once = pl.Buffered(1)
    pair = lambda o, c: (0, nc * o + c)
    out_spec = pl.BlockSpec((None, n1f * DFT_N, LANES), lambda o, c: (o, 0, c))
    out_shape = jax.ShapeDtypeStruct((HY_ORDER, n1f * DFT_N, D_GROUP), BF16)
    return pl.pallas_call(
        functools.partial(_hfilter_body, seq=seq),
        grid=(HY_ORDER, nc),
        in_specs=[
            pl.BlockSpec((seq, 2 * HY_HIDDEN), lambda o, c: (0, 0)),
            pl.BlockSpec((2 * HY_HIDDEN, 2 * LANES), pair),
            pl.BlockSpec((1, 2 * LANES), pair),
            pl.BlockSpec((n1f * DFT_N, LANES), lambda o, c: (0, 0), pipeline_mode=once),
            pl.BlockSpec((n1f * DFT_N, LANES), lambda o, c: (0, 0), pipeline_mode=once),
            pl.BlockSpec(memory_space=pltpu.SMEM), pl.BlockSpec(memory_space=pltpu.SMEM),
            pl.BlockSpec((2, DFT_N, 2 * DFT_N), lambda o, c: (0, 0, 0)),
        ],
        out_specs=[out_spec, out_spec],
        out_shape=[out_shape, out_shape],
        scratch_shapes=[pltpu.VMEM((2 * seq, LANES), F32), pltpu.VMEM((n1f * DFT_N, 2 * LANES), F32)],
        compiler_params=_params(("arbitrary", "arbitrary")),
        name="hyena_filter_spectrum",
    )(hid, w3, dec, tw[0], tw[1], *_pair_factors(n1f, False), fst)


def _trunk(x, p, fst):
    nb, seq, _ = x.shape
    x2d = x.reshape(nb * seq, D_MODEL)
    tw = _twiddle_table(2 * seq // DFT_N)
    feats = _filter_features(seq)
    vcol = 0
    g1col = D_GROUP // LANES
    g2col = 2 * D_GROUP // LANES
    for l in range(DEPTH):
        qkv, out_a, out_c, hz = _inmix(x2d, p["mix_norm_g"][l][None, :], p["w_in"], p["pool_w"],
                                       p["pool_scale"][l][None, :], p["sconv_w"][l], p["hy_short_w"][l], l, seq)
        hid = _filter_hidden(feats, p["hy_w1"][l], p["hy_b1"][l], p["hy_w2"][l], p["hy_b2"][l], p["hy_freq"][l])
        spec_r, spec_i = _hfilter(hid, p["hy_w3"][l], p["hy_decay"][l], tw, fst, seq)
        bias = p["hy_bias"][l]
        u1 = _hconv(hz, vcol, hz, g1col, spec_r, spec_i, 0, tw, fst, bias, nb, seq, F32)
        out_d = _hconv(u1, 0, hz, g2col, spec_r, spec_i, 1, tw, fst, bias, nb, seq, BF16)
        x2d = _attn_outproj(x2d, qkv, p["attn_sink"][l], out_a, out_c, out_d, p["w_out"], l, seq)
        x2d = _ffn(x2d, p["ffn_norm_g"][l][None, :], p["w_gate_up"], p["w_down"],
                   p["final_norm_g"][None, :], l, final_norm=(l == DEPTH - 1))
    return x2d.reshape(nb, seq, D_MODEL)


def kernel(x_prompt, x_sample, mix_norm_g, w_in, pool_w, pool_scale, attn_sink, sconv_w, hy_short_w, hy_w1, hy_b1, hy_w2, hy_b2, hy_w3, hy_freq, hy_decay, hy_bias, w_out, ffn_norm_g, w_gate_up, w_down, final_norm_g):
    p = dict(
        mix_norm_g=mix_norm_g, w_in=w_in.astype(BF16), pool_w=pool_w.astype(BF16), pool_scale=pool_scale,
        attn_sink=attn_sink, sconv_w=sconv_w, hy_short_w=hy_short_w, hy_w1=hy_w1, hy_b1=hy_b1, hy_w2=hy_w2,
        hy_b2=hy_b2, hy_w3=hy_w3, hy_freq=hy_freq, hy_decay=hy_decay, hy_bias=hy_bias,
        w_out=w_out, ffn_norm_g=ffn_norm_g, w_gate_up=w_gate_up,
        w_down=w_down, final_norm_g=final_norm_g,
    )
    fst = _dft_stack().astype(BF16)
    return (_trunk(x_prompt, p, fst), _trunk(x_sample, p, fst))
```
